```python
import jax, jax.numpy as jnp
from jax import lax
import numpy as np

D_MODEL = 1024
BATCH = 16
SEQ = 2048
DEPTH = 2

CHUNK = 64
PLE_DIM = 256
D_FF = 2816
EPS = 1e-6

RNN_WIDTH = 1024
RNN_BLOCKS = 16
RNN_BLOCK = RNN_WIDTH // RNN_BLOCKS
CONV_WIDTH = 4
LRU_C = 8.0

MLA_HEADS = 16
Q_LORA = 384
KV_LORA = 256
QK_NOPE = 64
QK_ROPE = 32
V_DIM = 64
ROPE_THETA = 10000.0
Q_BLOCK = 128

SGU_WIDTH = 1024
SGU_GROUPS = 8
SGU_GROUP = SGU_WIDTH // SGU_GROUPS
SGU_LEN = 128

N_BRANCH = 3
IN_SIZES = (RNN_WIDTH, RNN_WIDTH, Q_LORA, KV_LORA, QK_ROPE, 2 * SGU_WIDTH, N_BRANCH * D_MODEL)
W_IN_COLS = RNN_WIDTH * 2 + Q_LORA + KV_LORA + QK_ROPE + 2 * SGU_WIDTH + N_BRANCH * D_MODEL

kernel_name = "hybrid_rglru_mla_sgu_macaron_block"


def _split_points(sizes):
    pts, acc = [], 0
    for s in sizes[:-1]:
        acc += s
        pts.append(acc)
    return pts


def rms_norm(x, g):
    xf = x.astype(jnp.float32)
    y = xf * lax.rsqrt(jnp.mean(xf * xf, axis=-1, keepdims=True) + EPS)
    return (y * g.astype(jnp.float32)).astype(x.dtype)


def layer_norm(x, g, b):
    xf = x.astype(jnp.float32)
    mu = jnp.mean(xf, axis=-1, keepdims=True)
    var = jnp.mean(jnp.square(xf - mu), axis=-1, keepdims=True)
    y = (xf - mu) * lax.rsqrt(var + EPS)
    return (y * g.astype(jnp.float32) + b.astype(jnp.float32)).astype(x.dtype)


def swiglu_ffn(x, norm_g, w_gu, w_down):
    h = rms_norm(x, norm_g) @ w_gu
    gate, up = jnp.split(h, 2, axis=-1)
    return (jax.nn.silu(gate) * up) @ w_down


def causal_depthwise_conv(x, w, b):
    S = x.shape[1]
    xp = jnp.pad(x, ((0, 0), (CONV_WIDTH - 1, 0), (0, 0)))
    out = b
    for k in range(CONV_WIDTH):
        out = out + xp[:, k:k + S] * w[k]
    return out


def rg_lru(x, w_a, b_a, w_x, b_x, lam):
    B, S, _ = x.shape
    xb = x.reshape(B, S, RNN_BLOCKS, RNN_BLOCK)
    r = jax.nn.sigmoid(jnp.einsum('bsnc,ncd->bsnd', xb, w_a).reshape(B, S, RNN_WIDTH) + b_a)
    i = jax.nn.sigmoid(jnp.einsum('bsnc,ncd->bsnd', xb, w_x).reshape(B, S, RNN_WIDTH) + b_x)
    log_a = -LRU_C * r.astype(jnp.float32) * jax.nn.softplus(-lam.astype(jnp.float32))
    a = jnp.exp(log_a)
    mult = jnp.sqrt(-jnp.expm1(2.0 * log_a))
    first = (jnp.arange(S) == 0)[None, :, None]
    mult = jnp.where(first, jnp.ones((), jnp.float32), mult)
    u = mult * (i * x).astype(jnp.float32)

    def combine(c1, c2):
        a1, b1 = c1
        a2, b2 = c2
        return a1 * a2, a2 * b1 + b2

    _, h = lax.associative_scan(combine, (a, u), axis=1)
    return h.astype(x.dtype)


def apply_rope(x, cos, sin):
    half = QK_ROPE // 2
    x1 = x[..., :half].astype(jnp.float32)
    x2 = x[..., half:].astype(jnp.float32)
    out = jnp.concatenate([x1 * cos - x2 * sin, x1 * sin + x2 * cos], axis=-1)
    return out.astype(x.dtype)


def mla_attention(c_q, c_kv, k_r, q_norm, w_uq, kv_norm, w_ukv):
    B, S, _ = c_q.shape
    half = QK_ROPE // 2
    inv_freq = ROPE_THETA ** (-jnp.arange(half, dtype=jnp.float32) / half)
    ang = jnp.arange(S, dtype=jnp.float32)[:, None] * inv_freq[None, :]
    cos, sin = jnp.cos(ang), jnp.sin(ang)

    q = (rms_norm(c_q, q_norm) @ w_uq).reshape(B, S, MLA_HEADS, QK_NOPE + QK_ROPE)
    q_nope = q[..., :QK_NOPE]
    q_rope = apply_rope(q[..., QK_NOPE:], cos[None, :, None, :], sin[None, :, None, :])
    kv = (rms_norm(c_kv, kv_norm) @ w_ukv).reshape(B, S, MLA_HEADS, QK_NOPE + V_DIM)
    k_nope, v = kv[..., :QK_NOPE], kv[..., QK_NOPE:]
    k_rope = apply_rope(k_r, cos[None], sin[None])
    scale = (QK_NOPE + QK_ROPE) ** -0.5

    outs = []
    for qb in range(S // Q_BLOCK):
        q0, kend = qb * Q_BLOCK, (qb + 1) * Q_BLOCK
        s = (jnp.einsum('bqhd,bkhd->bhqk', q_nope[:, q0:kend], k_nope[:, :kend])
             + jnp.einsum('bqhr,bkr->bhqk', q_rope[:, q0:kend], k_rope[:, :kend]))
        s = s.astype(jnp.float32) * scale
        qi = q0 + jnp.arange(Q_BLOCK)
        kj = jnp.arange(kend)
        mask = (kj[None, :] // CHUNK) <= (qi[:, None] // CHUNK)
        s = jnp.where(mask[None, None], s, -1e30)
        pr = jax.nn.softmax(s, axis=-1).astype(v.dtype)
        outs.append(jnp.einsum('bhqk,bkhd->bqhd', pr, v[:, :kend]))
    o = jnp.concatenate(outs, axis=1)
    return o.reshape(B, S, MLA_HEADS * V_DIM)


def spatial_gating(z, ln_g, ln_b, w_s, b_s):
    B, S, _ = z.shape
    u, v = jnp.split(z, 2, axis=-1)
    v = layer_norm(v, ln_g, ln_b)
    vb = v.reshape(B, S // SGU_LEN, SGU_LEN, SGU_GROUPS, SGU_GROUP)
    t = jnp.arange(SGU_LEN)
    mask = (t[None, :] // CHUNK) <= (t[:, None] // CHUNK)
    w = jnp.where(mask[None], w_s, jnp.zeros((), w_s.dtype))
    sp = jnp.einsum('gts,bnsgc->bntgc', w, vb) + b_s.T[None, None, :, :, None]
    return u * sp.reshape(B, S, SGU_WIDTH)


def parallel_mixer(x, mix_norm, w_in, conv_w, conv_b, lru_w_a, lru_b_a, lru_w_x, lru_b_x,
                   lru_lambda, w_read_a, mla_q_norm, mla_w_uq, mla_kv_norm, mla_w_ukv,
                   w_read_b, sgu_norm_g, sgu_norm_b, sgu_w_s, sgu_b_s, w_read_c,
                   gate_bias, w_out):
    B, S, _ = x.shape
    n = rms_norm(x, mix_norm)
    z = n @ w_in
    xa, ga, cq, ckv, kr, zc, gl = jnp.split(z, _split_points(IN_SIZES), axis=-1)
    ya = rg_lru(causal_depthwise_conv(xa, conv_w, conv_b),
                lru_w_a, lru_b_a, lru_w_x, lru_b_x, lru_lambda) * jax.nn.gelu(ga)
    ya = ya @ w_read_a
    yb = mla_attention(cq, ckv, kr, mla_q_norm, mla_w_uq, mla_kv_norm, mla_w_ukv) @ w_read_b
    yc = spatial_gating(jax.nn.gelu(zc), sgu_norm_g, sgu_norm_b, sgu_w_s, sgu_b_s) @ w_read_c
    g = jax.nn.sigmoid(gl + gate_bias).reshape(B, S, N_BRANCH, D_MODEL)
    merged = g[:, :, 0] * ya + g[:, :, 1] * yb + g[:, :, 2] * yc
    return merged @ w_out


def setup_inputs(seed: int = 0) -> dict:
    key = jax.random.key(seed)
    ks = iter(jax.random.split(key, 64))
    f32 = jnp.float32

    def dense(shape, fan_in):
        return jax.random.normal(next(ks), shape, f32) * (fan_in ** -0.5)

    def gain(shape):
        return 1.0 + 0.05 * jax.random.normal(next(ks), shape, f32)

    def bias(shape, scale=0.01):
        return scale * jax.random.normal(next(ks), shape, f32)

    L = DEPTH
    a0 = jax.random.uniform(next(ks), (L, RNN_WIDTH), f32, 0.9, 0.999)
    inputs = {
        "x": jax.random.normal(next(ks), (BATCH, SEQ, D_MODEL), f32),
        "p": jax.random.normal(next(ks), (DEPTH, BATCH, SEQ, PLE_DIM), f32),
        "ffn1_norm": gain((L, D_MODEL)),
        "ffn1_w_gu": dense((L, D_MODEL, 2 * D_FF), D_MODEL),
        "ffn1_w_down": dense((L, D_FF, D_MODEL), D_FF),
        "mix_norm": gain((L, D_MODEL)),
        "w_in": dense((L, D_MODEL, W_IN_COLS), D_MODEL),
        "conv_w": dense((L, CONV_WIDTH, RNN_WIDTH), CONV_WIDTH),
        "conv_b": bias((L, RNN_WIDTH)),
        "lru_w_a": dense((L, RNN_BLOCKS, RNN_BLOCK, RNN_BLOCK), RNN_BLOCK),
        "lru_b_a": bias((L, RNN_WIDTH)),
        "lru_w_x": dense((L, RNN_BLOCKS, RNN_BLOCK, RNN_BLOCK), RNN_BLOCK),
        "lru_b_x": bias((L, RNN_WIDTH)),
        "lru_lambda": jnp.log(a0) - jnp.log1p(-a0),
        "w_read_a": dense((L, RNN_WIDTH, D_MODEL), RNN_WIDTH),
        "mla_q_norm": gain((L, Q_LORA)),
        "mla_w_uq": dense((L, Q_LORA, MLA_HEADS * (QK_NOPE + QK_ROPE)), Q_LORA),
        "mla_kv_norm": gain((L, KV_LORA)),
        "mla_w_ukv": dense((L, KV_LORA, MLA_HEADS * (QK_NOPE + V_DIM)), KV_LORA),
        "w_read_b": dense((L, MLA_HEADS * V_DIM, D_MODEL), MLA_HEADS * V_DIM),
        "sgu_norm_g": gain((L, SGU_WIDTH)),
        "sgu_norm_b": bias((L, SGU_WIDTH)),
        "sgu_w_s": dense((L, SGU_GROUPS, SGU_LEN, SGU_LEN), SGU_LEN),
        "sgu_b_s": 1.0 + bias((L, SGU_GROUPS, SGU_LEN), 0.1),
        "w_read_c": dense((L, SGU_WIDTH, D_MODEL), SGU_WIDTH),
        "gate_bias": bias((L, N_BRANCH * D_MODEL)),
        "w_out": dense((L, D_MODEL, D_MODEL), D_MODEL),
        "ffn2_norm": gain((L, D_MODEL)),
        "ffn2_w_gu": dense((L, D_MODEL, 2 * D_FF), D_MODEL),
        "ffn2_w_down": dense((L, D_FF, D_MODEL), D_FF),
        "ple_norm": gain((L, D_MODEL)),
        "ple_w_gate": dense((L, D_MODEL, D_MODEL), D_MODEL),
        "ple_w_proj": dense((L, PLE_DIM, D_MODEL), PLE_DIM),
        "final_norm": gain((D_MODEL,)),
    }
    return inputs


def reference(x, p, ffn1_norm, ffn1_w_gu, ffn1_w_down, mix_norm, w_in, conv_w, conv_b,
              lru_w_a, lru_b_a, lru_w_x, lru_b_x, lru_lambda, w_read_a, mla_q_norm,
              mla_w_uq, mla_kv_norm, mla_w_ukv, w_read_b, sgu_norm_g, sgu_norm_b,
              sgu_w_s, sgu_b_s, w_read_c, gate_bias, w_out, ffn2_norm, ffn2_w_gu,
              ffn2_w_down, ple_norm, ple_w_gate, ple_w_proj, final_norm):
    for i in range(DEPTH):
        x = x + 0.5 * swiglu_ffn(x, ffn1_norm[i], ffn1_w_gu[i], ffn1_w_down[i])
        x = x + parallel_mixer(x, mix_norm[i], w_in[i], conv_w[i], conv_b[i], lru_w_a[i],
                               lru_b_a[i], lru_w_x[i], lru_b_x[i], lru_lambda[i], w_read_a[i],
                               mla_q_norm[i], mla_w_uq[i], mla_kv_norm[i], mla_w_ukv[i],
                               w_read_b[i], sgu_norm_g[i], sgu_norm_b[i], sgu_w_s[i],
                               sgu_b_s[i], w_read_c[i], gate_bias[i], w_out[i])
        x = x + 0.5 * swiglu_ffn(x, ffn2_norm[i], ffn2_w_gu[i], ffn2_w_down[i])
        gate = jax.nn.sigmoid(rms_norm(x, ple_norm[i]) @ ple_w_gate[i])
        x = x + (p[i] @ ple_w_proj[i]) * gate
    return rms_norm(x, final_norm)
```

```python
import functools

import jax
import jax.numpy as jnp
from jax import lax
from jax.experimental import pallas as pl
from jax.experimental.pallas import tpu as pltpu

F32 = jnp.float32
BF = jnp.bfloat16

D_MODEL = 1024
D_FF = 2816
PLE_DIM = 256
EPS = 1e-6
CHUNK = 64

RNN_WIDTH = 1024
RNN_BLOCK = 64
CONV_WIDTH = 4
LRU_C = 8.0
LRU_TILE = 256

MLA_HEADS = 16
Q_LORA = 384
KV_LORA = 256
QK_NOPE = 64
QK_ROPE = 32
V_DIM = 64
ROPE_THETA = 10000.0
HEAD_PAD = 128

SGU_WIDTH = 1024
SGU_GROUPS = 8
SGU_GROUP = 128
SGU_LEN = 128

_O_XA, _O_GA, _O_CQ, _O_CKV, _O_KR, _O_ZC, _O_GL = 0, 1024, 2048, 2432, 2688, 2720, 4768
W_IN_COLS = 7840

VMEM_LIMIT = 56 * 1024 * 1024

FF_CHUNK = 256
TM_FFN = 512
TM_TOK = 512
TS_A = 512
TQ = 256


def _wspec(shape):
    nd = len(shape)
    return pl.BlockSpec(shape, lambda *_: (0,) * nd, pipeline_mode=pl.Buffered(1))


def _params(sem):
    return pltpu.CompilerParams(dimension_semantics=sem, vmem_limit_bytes=VMEM_LIMIT)


def _rms(x, g):
    return x * lax.rsqrt(jnp.mean(x * x, axis=-1, keepdims=True) + EPS) * g


def _dot(a, b):
    return jnp.dot(a, b, preferred_element_type=F32)


def _gelu(x):
    return 0.5 * x * (1.0 + jnp.tanh(0.7978845608028654 * (x + 0.044715 * (x * x * x))))


def _swiglu_residual(x, g_ref, wg_ref, wu_ref, wd_ref, acc_ref):
    nb = _rms(x, g_ref[...]).astype(BF)
    for c in range(D_FF // FF_CHUNK):
        sl = slice(c * FF_CHUNK, (c + 1) * FF_CHUNK)
        gate = _dot(nb, wg_ref[:, sl])
        up = _dot(nb, wu_ref[:, sl])
        h = (gate * jax.nn.sigmoid(gate) * up).astype(BF)
        d = _dot(h, wd_ref[sl, :])
        if c == 0:
            acc_ref[...] = d
        else:
            acc_ref[...] += d
    return x + 0.5 * acc_ref[...]


def _ffn1_kernel(x_ref, g_ref, wg_ref, wu_ref, wd_ref, mg_ref, o_ref, n_ref, acc_ref):
    y = _swiglu_residual(x_ref[...], g_ref, wg_ref, wu_ref, wd_ref, acc_ref)
    o_ref[...] = y
    n_ref[...] = _rms(y, mg_ref[...]).astype(BF)


def _ffn1(x, g, wg, wu, wd, mix_g):
    n_tok = x.shape[0]
    tm = min(TM_FFN, n_tok)
    tok = lambda i: (i, 0)
    return pl.pallas_call(
        _ffn1_kernel,
        grid=(n_tok // tm,),
        in_specs=[pl.BlockSpec((tm, D_MODEL), tok), _wspec(g.shape), _wspec(wg.shape), _wspec(wu.shape),
                  _wspec(wd.shape), _wspec(mix_g.shape)],
        out_specs=[pl.BlockSpec((tm, D_MODEL), tok), pl.BlockSpec((tm, D_MODEL), tok)],
        out_shape=[jax.ShapeDtypeStruct((n_tok, D_MODEL), F32), jax.ShapeDtypeStruct((n_tok, D_MODEL), BF)],
        scratch_shapes=[pltpu.VMEM((tm, D_MODEL), F32)],
        compiler_params=_params(("parallel",)),
        name="ffn1",
    )(x, g, wg, wu, wd, mix_g)


def _ffn2_kernel(x_ref, g_ref, wg_ref, wu_ref, wd_ref, p_ref, pg_ref, wpg_ref, wpp_ref, fg_ref, o_ref, acc_ref,
                 *, final):
    y = _swiglu_residual(x_ref[...], g_ref, wg_ref, wu_ref, wd_ref, acc_ref)
    gate = jax.nn.sigmoid(_dot(_rms(y, pg_ref[...]).astype(BF), wpg_ref[...]))
    y = y + _dot(p_ref[...].astype(BF), wpp_ref[...]) * gate
    if final:
        y = _rms(y, fg_ref[...])
    o_ref[...] = y


def _ffn2(x, g, wg, wu, wd, p, ple_g, w_pg, w_pp, final_g, final):
    n_tok = x.shape[0]
    tm = min(TM_FFN, n_tok)
    tok = lambda i: (i, 0)
    return pl.pallas_call(
        functools.partial(_ffn2_kernel, final=final),
        grid=(n_tok // tm,),
        in_specs=[pl.BlockSpec((tm, D_MODEL), tok), _wspec(g.shape), _wspec(wg.shape), _wspec(wu.shape),
                  _wspec(wd.shape), pl.BlockSpec((tm, PLE_DIM), tok), _wspec(ple_g.shape), _wspec(w_pg.shape),
                  _wspec(w_pp.shape), _wspec(final_g.shape)],
        out_specs=pl.BlockSpec((tm, D_MODEL), tok),
        out_shape=jax.ShapeDtypeStruct((n_tok, D_MODEL), F32),
        scratch_shapes=[pltpu.VMEM((tm, D_MODEL), F32)],
        compiler_params=_params(("parallel",)),
        name="ffn2_ple",
    )(x, g, wg, wu, wd, p, ple_g, w_pg, w_pp, final_g)


def _mixer_a_kernel(n_ref, wxa_ref, wga_ref, cw_ref, cb_ref, wa_ref, ba_ref, wx_ref, bx_ref, lam_ref, wr_ref,
                    o_ref, xbuf, abuf, ubuf, hcar, *, ts):
    t = pl.program_id(1)

    @pl.when(t == 0)
    def _():
        xbuf[0:8, :] = jnp.zeros((8, RNN_WIDTH), F32)
        hcar[...] = jnp.zeros((8, RNN_WIDTH), F32)

    nb = n_ref[0]
    xbuf[8:8 + ts, :] = _dot(nb, wxa_ref[...])
    xc = cb_ref[...] + xbuf[8:8 + ts, :] * cw_ref[3:4, :]
    for k in range(CONV_WIDTH - 1):
        xc = xc + xbuf[5 + k:5 + k + ts, :] * cw_ref[k:k + 1, :]
    xbuf[0:8, :] = xbuf[ts:ts + 8, :]

    xcb = xc.astype(BF)
    ra, ia = [], []
    for j in range(RNN_WIDTH // LRU_TILE):
        sl = slice(j * LRU_TILE, (j + 1) * LRU_TILE)
        ra.append(_dot(xcb[:, sl], wa_ref[j]))
        ia.append(_dot(xcb[:, sl], wx_ref[j]))
    r = jax.nn.sigmoid(jnp.concatenate(ra, axis=1) + ba_ref[...])
    gi = jax.nn.sigmoid(jnp.concatenate(ia, axis=1) + bx_ref[...])
    neg_lam = -lam_ref[...]
    softplus = jnp.maximum(neg_lam, 0.0) + jnp.log1p(jnp.exp(-jnp.abs(neg_lam)))
    log_a = (-LRU_C) * r * softplus
    a = jnp.exp(log_a)
    th = jnp.tanh(log_a)
    mult = jnp.sqrt(-2.0 * th / (1.0 - th))
    row = lax.broadcasted_iota(jnp.int32, (ts, 1), 0) + t * ts
    mult = jnp.where(row == 0, 1.0, mult)
    abuf[...] = a
    ubuf[...] = mult * (gi * xc)

    sub = lax.broadcasted_iota(jnp.int32, (8, RNN_WIDTH), 0)

    def group(i, h_prev):
        r0 = pl.multiple_of(i * 8, 8)
        av = abuf[pl.ds(r0, 8), :]
        bv = ubuf[pl.ds(r0, 8), :]
        for d in (1, 2, 4):
            keep = sub >= d
            a_sh = jnp.where(keep, pltpu.roll(av, d, 0), 1.0)
            b_sh = jnp.where(keep, pltpu.roll(bv, d, 0), 0.0)
            bv = av * b_sh + bv
            av = av * a_sh
        h = av * h_prev + bv
        ubuf[pl.ds(r0, 8), :] = h
        return jnp.broadcast_to(h[7:8, :], (8, RNN_WIDTH))

    hcar[...] = lax.fori_loop(0, ts // 8, group, hcar[...], unroll=2)

    ga = _dot(nb, wga_ref[...])
    o_ref[0] = _dot((ubuf[...] * _gelu(ga)).astype(BF), wr_ref[...]).astype(BF)


def _mixer_a(n3, wxa, wga, conv_w, conv_b, wa, ba, wx, bx, lam, w_read):
    b, s, _ = n3.shape
    ts = min(TS_A, s)
    tok = lambda i, j: (i, j, 0)
    ws = [wxa, wga, conv_w, conv_b, wa, ba, wx, bx, lam, w_read]
    return pl.pallas_call(
        functools.partial(_mixer_a_kernel, ts=ts),
        grid=(b, s // ts),
        in_specs=[pl.BlockSpec((1, ts, D_MODEL), tok)] + [_wspec(w.shape) for w in ws],
        out_specs=pl.BlockSpec((1, ts, D_MODEL), tok),
        out_shape=jax.ShapeDtypeStruct((b, s, D_MODEL), BF),
        scratch_shapes=[pltpu.VMEM((ts + 8, RNN_WIDTH), F32), pltpu.VMEM((ts, RNN_WIDTH), F32),
                        pltpu.VMEM((ts, RNN_WIDTH), F32), pltpu.VMEM((8, RNN_WIDTH), F32)],
        compiler_params=_params(("parallel", "arbitrary")),
        name="mixer_a",
    )(n3, *ws)


def _mla_proj_kernel(n_ref, wb_ref, qg_ref, kg_ref, wq_ref, wqr_ref, wk_ref, wv_ref, cos_ref, sin_ref,
                     q_ref, k_ref, v_ref):
    c = _dot(n_ref[...], wb_ref[...])
    cq = _rms(c[:, 0:Q_LORA], qg_ref[...]).astype(BF)
    ckv = _rms(c[:, Q_LORA:Q_LORA + KV_LORA], kg_ref[...]).astype(BF)
    o_kr = Q_LORA + KV_LORA
    cos = cos_ref[...]
    sin = sin_ref[...]
    k_rope = c[:, o_kr:o_kr + HEAD_PAD] * cos + c[:, o_kr + HEAD_PAD:o_kr + 2 * HEAD_PAD] * sin
    scale = (QK_NOPE + QK_ROPE) ** -0.5
    q = _dot(cq, wq_ref[...])
    qr = _dot(cq, wqr_ref[...])
    k = _dot(ckv, wk_ref[...])
    for h in range(MLA_HEADS):
        sl = slice(h * HEAD_PAD, (h + 1) * HEAD_PAD)
        q_ref[:, sl] = ((q[:, sl] * cos + qr[:, sl] * sin) * scale).astype(BF)
        k_ref[:, sl] = (k[:, sl] + k_rope).astype(BF)
    v_ref[...] = _dot(ckv, wv_ref[...]).astype(BF)


def _mla_proj(n2, wb, q_g, kv_g, wq, wqr, wk, wv, cos_t, sin_t, seq):
    n_tok = n2.shape[0]
    tm = min(TM_TOK, seq)
    per_seq = seq // tm
    tok = lambda i: (i, 0)
    pos = lambda i: (i % per_seq, 0)
    ws = [wb, q_g, kv_g, wq, wqr, wk, wv]
    hp = MLA_HEADS * HEAD_PAD
    return pl.pallas_call(
        _mla_proj_kernel,
        grid=(n_tok // tm,),
        in_specs=[pl.BlockSpec((tm, D_MODEL), tok)] + [_wspec(w.shape) for w in ws]
        + [pl.BlockSpec((tm, HEAD_PAD), pos), pl.BlockSpec((tm, HEAD_PAD), pos)],
        out_specs=[pl.BlockSpec((tm, hp), tok), pl.BlockSpec((tm, hp), tok),
                   pl.BlockSpec((tm, MLA_HEADS * V_DIM), tok)],
        out_shape=[jax.ShapeDtypeStruct((n_tok, hp), BF), jax.ShapeDtypeStruct((n_tok, hp), BF),
                   jax.ShapeDtypeStruct((n_tok, MLA_HEADS * V_DIM), BF)],
        compiler_params=_params(("parallel",)),
        name="mla_proj",
    )(n2, *ws, cos_t, sin_t)


def _mla_attn_kernel(q_ref, k_ref, v_ref, o_ref, *, tq):
    qi = pl.program_id(2)
    row_chunk = lax.broadcasted_iota(jnp.int32, (tq, tq), 0) // CHUNK
    col_chunk = lax.broadcasted_iota(jnp.int32, (tq, tq), 1) // CHUNK
    diag_mask = col_chunk <= row_chunk
    outs = []
    for hh in range(2):
        q = q_ref[0, :, hh * HEAD_PAD:(hh + 1) * HEAD_PAD]

        def step(j, carry, masked):
            m, l, acc = carry
            k0 = pl.multiple_of(j * tq, tq)
            kt = k_ref[0, pl.ds(k0, tq), hh * HEAD_PAD:(hh + 1) * HEAD_PAD]
            vt = v_ref[0, pl.ds(k0, tq), hh * V_DIM:(hh + 1) * V_DIM]
            s = lax.dot_general(q, kt, (((1,), (1,)), ((), ())), preferred_element_type=F32)
            if masked:
                s = jnp.where(diag_mask, s, -1e30)
            m_new = jnp.maximum(m, jnp.max(s, axis=-1, keepdims=True))
            alpha = jnp.exp(m - m_new)
            p = jnp.exp(s - m_new)
            l = alpha * l + jnp.sum(p, axis=-1, keepdims=True)
            acc = alpha * acc + _dot(p.astype(BF), vt)
            return m_new, l, acc

        init = (jnp.full((tq, 1), -1e30, F32), jnp.zeros((tq, 1), F32), jnp.zeros((tq, V_DIM), F32))
        carry = lax.fori_loop(0, qi, functools.partial(step, masked=False), init)
        m, l, acc = step(qi, carry, True)
        outs.append(acc / l)
    o_ref[0] = jnp.concatenate(outs, axis=1).astype(BF)


def _mla_attn(q3, k3, v3):
    b, s, _ = q3.shape
    tq = min(TQ, s)
    return pl.pallas_call(
        functools.partial(_mla_attn_kernel, tq=tq),
        grid=(b, MLA_HEADS // 2, s // tq),
        in_specs=[pl.BlockSpec((1, tq, 2 * HEAD_PAD), lambda i, h, j: (i, j, h)),
                  pl.BlockSpec((1, s, 2 * HEAD_PAD), lambda i, h, j: (i, 0, h)),
                  pl.BlockSpec((1, s, 2 * V_DIM), lambda i, h, j: (i, 0, h))],
        out_specs=pl.BlockSpec((1, tq, 2 * V_DIM), lambda i, h, j: (i, j, h)),
        out_shape=jax.ShapeDtypeStruct((b, s, MLA_HEADS * V_DIM), BF),
        compiler_params=_params(("parallel", "parallel", "arbitrary")),
        name="mla_attn",
    )(q3, k3, v3)


def _sgu_kernel(n_ref, wc_ref, lg_ref, lb_ref, ws_ref, bs_ref, wr_ref, o_ref, gbuf, *, tm):
    z = _gelu(_dot(n_ref[...], wc_ref[...]))
    u = z[:, :SGU_WIDTH]
    v = z[:, SGU_WIDTH:]
    mu = jnp.mean(v, axis=-1, keepdims=True)
    vc = v - mu
    var = jnp.mean(vc * vc, axis=-1, keepdims=True)
    vn = (vc * lax.rsqrt(var + EPS) * lg_ref[...] + lb_ref[...]).astype(BF)
    t_out = lax.broadcasted_iota(jnp.int32, (SGU_LEN, SGU_LEN), 0) // CHUNK
    s_in = lax.broadcasted_iota(jnp.int32, (SGU_LEN, SGU_LEN), 1) // CHUNK
    mask = s_in <= t_out
    for g in range(SGU_GROUPS):
        w = jnp.where(mask, ws_ref[g], 0.0).astype(BF)
        bias = bs_ref[:, g:g + 1]
        cs = slice(g * SGU_GROUP, (g + 1) * SGU_GROUP)
        for r in range(tm // SGU_LEN):
            rs = slice(r * SGU_LEN, (r + 1) * SGU_LEN)
            sp = _dot(w, vn[rs, cs]) + bias
            gbuf[rs, cs] = (u[rs, cs] * sp).astype(BF)
    o_ref[...] = _dot(gbuf[...], wr_ref[...]).astype(BF)


def _sgu(n2, wc, ln_g, ln_b, w_s, b_st, w_read, seq):
    n_tok = n2.shape[0]
    tm = min(TM_TOK, seq)
    tok = lambda i: (i, 0)
    ws = [wc, ln_g, ln_b, w_s, b_st, w_read]
    return pl.pallas_call(
        functools.partial(_sgu_kernel, tm=tm),
        grid=(n_tok // tm,),
        in_specs=[pl.BlockSpec((tm, D_MODEL), tok)] + [_wspec(w.shape) for w in ws],
        out_specs=pl.BlockSpec((tm, D_MODEL), tok),
        out_shape=jax.ShapeDtypeStruct((n_tok, D_MODEL), BF),
        scratch_shapes=[pltpu.VMEM((tm, SGU_WIDTH), BF)],
        compiler_params=_params(("parallel",)),
        name="sgu",
    )(n2, *ws)


def _merge_kernel(x_ref, n_ref, ya_ref, ob_ref, yc_ref, wg_ref, gb_ref, wrb_ref, wo_ref, o_ref):
    g = jax.nn.sigmoid(_dot(n_ref[...], wg_ref[...]) + gb_ref[...])
    yb = _dot(ob_ref[...], wrb_ref[...])
    merged = (g[:, 0:D_MODEL] * ya_ref[...].astype(F32) + g[:, D_MODEL:2 * D_MODEL] * yb
              + g[:, 2 * D_MODEL:3 * D_MODEL] * yc_ref[...].astype(F32))
    o_ref[...] = x_ref[...] + _dot(merged.astype(BF), wo_ref[...])


def _merge(x, n2, ya, ob, yc, wg, gb, w_read_b, w_out):
    n_tok = x.shape[0]
    tm = min(TM_TOK, n_tok)
    tok = lambda i: (i, 0)
    ws = [wg, gb, w_read_b, w_out]
    return pl.pallas_call(
        _merge_kernel,
        grid=(n_tok // tm,),
        in_specs=[pl.BlockSpec((tm, D_MODEL), tok)] * 5 + [_wspec(w.shape) for w in ws],
        out_specs=pl.BlockSpec((tm, D_MODEL), tok),
        out_shape=jax.ShapeDtypeStruct((n_tok, D_MODEL), F32),
        compiler_params=_params(("parallel",)),
        name="merge",
    )(x, n2, ya, ob, yc, *ws)


def _rope_tables(seq):
    half = QK_ROPE // 2
    inv_freq = ROPE_THETA ** (-jnp.arange(half, dtype=F32) / half)
    ang = jnp.arange(seq, dtype=F32)[:, None] * inv_freq[None, :]
    cos, sin = jnp.cos(ang), jnp.sin(ang)
    ones = jnp.ones((seq, QK_NOPE), F32)
    zpad = jnp.zeros((seq, HEAD_PAD - QK_NOPE - QK_ROPE), F32)
    cos_t = jnp.concatenate([ones, cos, cos, zpad], axis=1)
    sin_t = jnp.concatenate([jnp.zeros((seq, QK_NOPE), F32), sin, sin, zpad], axis=1)
    return cos_t, sin_t


def _rot_half_cols(w):
    half = QK_ROPE // 2
    return jnp.concatenate([-w[..., half:], w[..., :half]], axis=-1)


def _pad_head(nope, rope):
    width = nope.shape[-1] + rope.shape[-1]
    z = jnp.zeros(nope.shape[:-1] + (HEAD_PAD - width,), nope.dtype)
    return jnp.concatenate([nope, rope, z], axis=-1)


def _block_diag_tiles(w):
    per = LRU_TILE // RNN_BLOCK
    w4 = w.reshape(-1, per, RNN_BLOCK, RNN_BLOCK)
    eye = jnp.eye(per, dtype=w.dtype)
    return jnp.einsum("jacd,ab->jacbd", w4, eye).reshape(-1, LRU_TILE, LRU_TILE)


def kernel(x, p, ffn1_norm, ffn1_w_gu, ffn1_w_down, mix_norm, w_in, conv_w, conv_b, lru_w_a, lru_b_a, lru_w_x, lru_b_x, lru_lambda, w_read_a, mla_q_norm, mla_w_uq, mla_kv_norm, mla_w_ukv, w_read_b, sgu_norm_g, sgu_norm_b, sgu_w_s, sgu_b_s, w_read_c, gate_bias, w_out, ffn2_norm, ffn2_w_gu, ffn2_w_down, ple_norm, ple_w_gate, ple_w_proj, final_norm):
    b, s, d = x.shape
    depth = p.shape[0]
    n_tok = b * s
    row = lambda a: a.reshape(1, -1)
    cos_t, sin_t = _rope_tables(s)
    xf = x.reshape(n_tok, d)
    for i in range(depth):
        wi = w_in[i]
        xf, n2 = _ffn1(xf, row(ffn1_norm[i]), ffn1_w_gu[i][:, :D_FF].astype(BF), ffn1_w_gu[i][:, D_FF:].astype(BF),
                       ffn1_w_down[i].astype(BF), row(mix_norm[i]))
        n3 = n2.reshape(b, s, d)

        ya = _mixer_a(n3, wi[:, _O_XA:_O_GA].astype(BF), wi[:, _O_GA:_O_CQ].astype(BF), conv_w[i], row(conv_b[i]),
                      _block_diag_tiles(lru_w_a[i]).astype(BF), row(lru_b_a[i]),
                      _block_diag_tiles(lru_w_x[i]).astype(BF), row(lru_b_x[i]), row(lru_lambda[i]),
                      w_read_a[i].astype(BF)).reshape(n_tok, d)

        w_kr = wi[:, _O_KR:_O_ZC]
        zn = jnp.zeros((d, QK_NOPE), F32)
        wb = jnp.concatenate([wi[:, _O_CQ:_O_KR], _pad_head(zn, w_kr), _pad_head(zn, _rot_half_cols(w_kr))],
                             axis=1).astype(BF)
        wq = mla_w_uq[i].reshape(Q_LORA, MLA_HEADS, QK_NOPE + QK_ROPE)
        wq_n, wq_r = wq[..., :QK_NOPE], wq[..., QK_NOPE:]
        wq_pad = _pad_head(wq_n, wq_r).reshape(Q_LORA, -1).astype(BF)
        wq_rot = _pad_head(jnp.zeros_like(wq_n), _rot_half_cols(wq_r)).reshape(Q_LORA, -1).astype(BF)
        wkv = mla_w_ukv[i].reshape(KV_LORA, MLA_HEADS, QK_NOPE + V_DIM)
        wk_pad = _pad_head(wkv[..., :QK_NOPE], jnp.zeros((KV_LORA, MLA_HEADS, QK_ROPE), F32))
        wk_pad = wk_pad.reshape(KV_LORA, -1).astype(BF)
        wv = wkv[..., QK_NOPE:].reshape(KV_LORA, -1).astype(BF)
        q2, k2, v2 = _mla_proj(n2, wb, row(mla_q_norm[i]), row(mla_kv_norm[i]), wq_pad, wq_rot, wk_pad, wv,
                               cos_t, sin_t, s)
        ob = _mla_attn(q2.reshape(b, s, -1), k2.reshape(b, s, -1), v2.reshape(b, s, -1)).reshape(n_tok, -1)

        yc = _sgu(n2, wi[:, _O_ZC:_O_GL].astype(BF), row(sgu_norm_g[i]), row(sgu_norm_b[i]), sgu_w_s[i],
                  sgu_b_s[i].T, w_read_c[i].astype(BF), s)

        xf = _merge(xf, n2, ya, ob, yc, wi[:, _O_GL:].astype(BF), row(gate_bias[i]), w_read_b[i].astype(BF),
                    w_out[i].astype(BF))

        xf = _ffn2(xf, row(ffn2_norm[i]), ffn2_w_gu[i][:, :D_FF].astype(BF), ffn2_w_gu[i][:, D_FF:].astype(BF),
                   ffn2_w_down[i].astype(BF), p[i].reshape(n_tok, PLE_DIM), row(ple_norm[i]),
                   ple_w_gate[i].astype(BF), ple_w_proj[i].astype(BF), row(final_norm), final=(i == depth - 1))
    return xf.reshape(b, s, d)
```

```python
import functools

import jax
import jax.numpy as jnp
from jax import lax
from jax.experimental import pallas as pl
from jax.experimental.pallas import tpu as pltpu

F32 = jnp.float32
BF = jnp.bfloat16

D_MODEL = 1024
D_FF = 2816
PLE_DIM = 256
EPS = 1e-6
CHUNK = 64

RNN_WIDTH = 1024
RNN_BLOCK = 64
CONV_WIDTH = 4
LRU_C = 8.0
LRU_TILE = 256

MLA_HEADS = 16
Q_LORA = 384
KV_LORA = 256
QK_NOPE = 64
QK_ROPE = 32
V_DIM = 64
ROPE_THETA = 10000.0
HEAD_PAD = 128

SGU_WIDTH = 1024
SGU_GROUPS = 8
SGU_GROUP = 128
SGU_LEN = 128

_O_XA, _O_GA, _O_CQ, _O_CKV, _O_KR, _O_ZC, _O_GL = 0, 1024, 2048, 2432, 2688, 2720, 4768
W_IN_COLS = 7840

VMEM_LIMIT = 56 * 1024 * 1024

FF_CHUNK = 256
TM_FFN = 512
TM_TOK = 512
TS_A = 512
TQ = 512
ATTN_HEADS_PER_STEP = 2
LOG2_E = 1.4426950408889634


def _wspec(shape):
    nd = len(shape)
    return pl.BlockSpec(shape, lambda *_: (0,) * nd, pipeline_mode=pl.Buffered(1))


def _params(sem):
    return pltpu.CompilerParams(dimension_semantics=sem, vmem_limit_bytes=VMEM_LIMIT)


def _rms(x, g):
    return x * lax.rsqrt(jnp.mean(x * x, axis=-1, keepdims=True) + EPS) * g


def _dot(a, b):
    return jnp.dot(a, b, preferred_element_type=F32)


def _gelu(x):
    return 0.5 * x * (1.0 + jnp.tanh(0.7978845608028654 * (x + 0.044715 * (x * x * x))))


def _swiglu_residual(x, g_ref, wg_ref, wu_ref, wd_ref, acc_ref):
    nb = _rms(x, g_ref[...]).astype(BF)
    for c in range(D_FF // FF_CHUNK):
        sl = slice(c * FF_CHUNK, (c + 1) * FF_CHUNK)
        gate = _dot(nb, wg_ref[:, sl])
        up = _dot(nb, wu_ref[:, sl])
        h = (gate * jax.nn.sigmoid(gate) * up).astype(BF)
        d = _dot(h, wd_ref[sl, :])
        if c == 0:
            acc_ref[...] = d
        else:
            acc_ref[...] += d
    return x + 0.5 * acc_ref[...]


def _ffn1_kernel(x_ref, g_ref, wg_ref, wu_ref, wd_ref, mg_ref, o_ref, n_ref, acc_ref):
    y = _swiglu_residual(x_ref[...], g_ref, wg_ref, wu_ref, wd_ref, acc_ref)
    o_ref[...] = y
    n_ref[...] = _rms(y, mg_ref[...]).astype(BF)


def _ffn1(x, g, wg, wu, wd, mix_g):
    n_tok = x.shape[0]
    tm = min(TM_FFN, n_tok)
    tok = lambda i: (i, 0)
    return pl.pallas_call(
        _ffn1_kernel,
        grid=(n_tok // tm,),
        in_specs=[pl.BlockSpec((tm, D_MODEL), tok), _wspec(g.shape), _wspec(wg.shape), _wspec(wu.shape),
                  _wspec(wd.shape), _wspec(mix_g.shape)],
        out_specs=[pl.BlockSpec((tm, D_MODEL), tok), pl.BlockSpec((tm, D_MODEL), tok)],
        out_shape=[jax.ShapeDtypeStruct((n_tok, D_MODEL), F32), jax.ShapeDtypeStruct((n_tok, D_MODEL), BF)],
        scratch_shapes=[pltpu.VMEM((tm, D_MODEL), F32)],
        compiler_params=_params(("parallel",)),
        name="ffn1",
    )(x, g, wg, wu, wd, mix_g)


def _ffn2_kernel(x_ref, g_ref, wg_ref, wu_ref, wd_ref, p_ref, pg_ref, wpg_ref, wpp_ref, fg_ref, o_ref, acc_ref,
                 *, final):
    y = _swiglu_residual(x_ref[...], g_ref, wg_ref, wu_ref, wd_ref, acc_ref)
    gate = jax.nn.sigmoid(_dot(_rms(y, pg_ref[...]).astype(BF), wpg_ref[...]))
    y = y + _dot(p_ref[...].astype(BF), wpp_ref[...]) * gate
    if final:
        y = _rms(y, fg_ref[...])
    o_ref[...] = y


def _ffn2(x, g, wg, wu, wd, p, ple_g, w_pg, w_pp, final_g, final):
    n_tok = x.shape[0]
    tm = min(TM_FFN, n_tok)
    tok = lambda i: (i, 0)
    return pl.pallas_call(
        functools.partial(_ffn2_kernel, final=final),
        grid=(n_tok // tm,),
        in_specs=[pl.BlockSpec((tm, D_MODEL), tok), _wspec(g.shape), _wspec(wg.shape), _wspec(wu.shape),
                  _wspec(wd.shape), pl.BlockSpec((tm, PLE_DIM), tok), _wspec(ple_g.shape), _wspec(w_pg.shape),
                  _wspec(w_pp.shape), _wspec(final_g.shape)],
        out_specs=pl.BlockSpec((tm, D_MODEL), tok),
        out_shape=jax.ShapeDtypeStruct((n_tok, D_MODEL), F32),
        scratch_shapes=[pltpu.VMEM((tm, D_MODEL), F32)],
        compiler_params=_params(("parallel",)),
        name="ffn2_ple",
    )(x, g, wg, wu, wd, p, ple_g, w_pg, w_pp, final_g)


def _mixer_a_kernel(n_ref, wxa_ref, wga_ref, cw_ref, cb_ref, wa_ref, ba_ref, wx_ref, bx_ref, lam_ref, wr_ref,
                    o_ref, xbuf, abuf, ubuf, hcar, *, ts):
    t = pl.program_id(1)

    @pl.when(t == 0)
    def _():
        xbuf[0:8, :] = jnp.zeros((8, RNN_WIDTH), F32)
        hcar[...] = jnp.zeros((8, RNN_WIDTH), F32)

    nb = n_ref[0]
    xbuf[8:8 + ts, :] = _dot(nb, wxa_ref[...])
    xc = cb_ref[...] + xbuf[8:8 + ts, :] * cw_ref[3:4, :]
    for k in range(CONV_WIDTH - 1):
        xc = xc + xbuf[5 + k:5 + k + ts, :] * cw_ref[k:k + 1, :]
    xbuf[0:8, :] = xbuf[ts:ts + 8, :]

    xcb = xc.astype(BF)
    ra, ia = [], []
    for j in range(RNN_WIDTH // LRU_TILE):
        sl = slice(j * LRU_TILE, (j + 1) * LRU_TILE)
        ra.append(_dot(xcb[:, sl], wa_ref[j]))
        ia.append(_dot(xcb[:, sl], wx_ref[j]))
    r = jax.nn.sigmoid(jnp.concatenate(ra, axis=1) + ba_ref[...])
    gi = jax.nn.sigmoid(jnp.concatenate(ia, axis=1) + bx_ref[...])
    neg_lam = -lam_ref[...]
    softplus = jnp.maximum(neg_lam, 0.0) + jnp.log1p(jnp.exp(-jnp.abs(neg_lam)))
    log_a = (-LRU_C) * r * softplus
    a = jnp.exp(log_a)
    th = jnp.tanh(log_a)
    mult = jnp.sqrt(-2.0 * th / (1.0 - th))
    row = lax.broadcasted_iota(jnp.int32, (ts, 1), 0) + t * ts
    mult = jnp.where(row == 0, 1.0, mult)
    abuf[...] = a
    ubuf[...] = mult * (gi * xc)

    sub = lax.broadcasted_iota(jnp.int32, (8, RNN_WIDTH), 0)

    def group(i, h_prev):
        r0 = pl.multiple_of(i * 8, 8)
        av = abuf[pl.ds(r0, 8), :]
        bv = ubuf[pl.ds(r0, 8), :]
        for d in (1, 2, 4):
            keep = sub >= d
            a_sh = jnp.where(keep, pltpu.roll(av, d, 0), 1.0)
            b_sh = jnp.where(keep, pltpu.roll(bv, d, 0), 0.0)
            bv = av * b_sh + bv
            av = av * a_sh
        h = av * h_prev + bv
        ubuf[pl.ds(r0, 8), :] = h
        return jnp.broadcast_to(h[7:8, :], (8, RNN_WIDTH))

    hcar[...] = lax.fori_loop(0, ts // 8, group, hcar[...], unroll=2)

    ga = _dot(nb, wga_ref[...])
    o_ref[0] = _dot((ubuf[...] * _gelu(ga)).astype(BF), wr_ref[...]).astype(BF)


def _mixer_a(n3, wxa, wga, conv_w, conv_b, wa, ba, wx, bx, lam, w_read):
    b, s, _ = n3.shape
    ts = min(TS_A, s)
    tok = lambda i, j: (i, j, 0)
    ws = [wxa, wga, conv_w, conv_b, wa, ba, wx, bx, lam, w_read]
    return pl.pallas_call(
        functools.partial(_mixer_a_kernel, ts=ts),
        grid=(b, s // ts),
        in_specs=[pl.BlockSpec((1, ts, D_MODEL), tok)] + [_wspec(w.shape) for w in ws],
        out_specs=pl.BlockSpec((1, ts, D_MODEL), tok),
        out_shape=jax.ShapeDtypeStruct((b, s, D_MODEL), BF),
        scratch_shapes=[pltpu.VMEM((ts + 8, RNN_WIDTH), F32), pltpu.VMEM((ts, RNN_WIDTH), F32),
                        pltpu.VMEM((ts, RNN_WIDTH), F32), pltpu.VMEM((8, RNN_WIDTH), F32)],
        compiler_params=_params(("parallel", "arbitrary")),
        name="mixer_a",
    )(n3, *ws)


def _mla_proj_kernel(n_ref, wb_ref, qg_ref, kg_ref, wq_ref, wqr_ref, wk_ref, wvt_ref, cos_ref, sin_ref,
                     q_ref, k_ref, vt_ref):
    c = _dot(n_ref[...], wb_ref[...])
    cq = _rms(c[:, 0:Q_LORA], qg_ref[...]).astype(BF)
    ckv = _rms(c[:, Q_LORA:Q_LORA + KV_LORA], kg_ref[...]).astype(BF)
    o_kr = Q_LORA + KV_LORA
    cos = cos_ref[...]
    sin = sin_ref[...]
    k_rope = c[:, o_kr:o_kr + HEAD_PAD] * cos + c[:, o_kr + HEAD_PAD:o_kr + 2 * HEAD_PAD] * sin
    scale = (QK_NOPE + QK_ROPE) ** -0.5 * LOG2_E
    q = _dot(cq, wq_ref[...])
    qr = _dot(cq, wqr_ref[...])
    k = _dot(ckv, wk_ref[...])
    for h in range(MLA_HEADS):
        sl = slice(h * HEAD_PAD, (h + 1) * HEAD_PAD)
        q_ref[:, sl] = ((q[:, sl] * cos + qr[:, sl] * sin) * scale).astype(BF)
        k_ref[:, sl] = (k[:, sl] + k_rope).astype(BF)
    vt = lax.dot_general(wvt_ref[...], ckv, (((1,), (1,)), ((), ())), preferred_element_type=F32)
    head_row = lax.broadcasted_iota(jnp.int32, vt.shape, 0) % HEAD_PAD
    vt_ref[0] = jnp.where(head_row == V_DIM, 1.0, vt).astype(BF)


def _mla_proj(n2, wb, q_g, kv_g, wq, wqr, wk, wvt, cos_t, sin_t, seq):
    n_tok = n2.shape[0]
    tm = min(TM_TOK, seq)
    per_seq = seq // tm
    tok = lambda i: (i, 0)
    pos = lambda i: (i % per_seq, 0)
    ws = [wb, q_g, kv_g, wq, wqr, wk, wvt]
    hp = MLA_HEADS * HEAD_PAD
    return pl.pallas_call(
        _mla_proj_kernel,
        grid=(n_tok // tm,),
        in_specs=[pl.BlockSpec((tm, D_MODEL), tok)] + [_wspec(w.shape) for w in ws]
        + [pl.BlockSpec((tm, HEAD_PAD), pos), pl.BlockSpec((tm, HEAD_PAD), pos)],
        out_specs=[pl.BlockSpec((tm, hp), tok), pl.BlockSpec((tm, hp), tok),
                   pl.BlockSpec((1, hp, tm), lambda i: (i // per_seq, 0, i % per_seq))],
        out_shape=[jax.ShapeDtypeStruct((n_tok, hp), BF), jax.ShapeDtypeStruct((n_tok, hp), BF),
                   jax.ShapeDtypeStruct((n_tok // seq, hp, seq), BF)],
        compiler_params=_params(("parallel",)),
        name="mla_proj",
    )(n2, *ws, cos_t, sin_t)


def _mla_attn_kernel(q_ref, k_ref, vt_ref, o_ref, *, tq, nq):
    qi = pl.program_id(2)
    key_chunk = lax.broadcasted_iota(jnp.int32, (tq, tq), 0) // CHUNK
    qry_chunk = lax.broadcasted_iota(jnp.int32, (tq, tq), 1) // CHUNK
    diag_mask = key_chunk <= qry_chunk
    nt = (((1,), (1,)), ((), ()))

    for c in range(nq):
        @pl.when(qi == c)
        def _():
            d0 = c * tq
            for hh in range(ATTN_HEADS_PER_STEP):
                hs = slice(hh * HEAD_PAD, (hh + 1) * HEAD_PAD)
                q = q_ref[0, :, hs]
                s_d = lax.dot_general(k_ref[0, d0:d0 + tq, hs], q, nt, preferred_element_type=F32)
                s_d = jnp.where(diag_mask, s_d, -1e30)
                m = jnp.max(s_d, axis=0, keepdims=True)
                if c > 0:
                    s_o = lax.dot_general(k_ref[0, 0:d0, hs], q, nt, preferred_element_type=F32)
                    m = jnp.maximum(m, jnp.max(s_o, axis=0, keepdims=True))
                    p_o = jnp.exp2(s_o - m).astype(BF)
                p_d = jnp.exp2(s_d - m).astype(BF)
                acc = _dot(vt_ref[0, hs, d0:d0 + tq], p_d)
                if c > 0:
                    acc = acc + _dot(vt_ref[0, hs, 0:d0], p_o)
                o_t = acc[0:V_DIM, :] * (1.0 / acc[V_DIM:V_DIM + 1, :])
                o_ref[0, hh * V_DIM:(hh + 1) * V_DIM, :] = o_t.astype(BF)


def _mla_attn(q3, k3, vt3):
    b, s, _ = q3.shape
    tq = min(TQ, s)
    nq = s // tq
    hb = ATTN_HEADS_PER_STEP
    return pl.pallas_call(
        functools.partial(_mla_attn_kernel, tq=tq, nq=nq),
        grid=(b, MLA_HEADS // hb, nq),
        in_specs=[pl.BlockSpec((1, tq, hb * HEAD_PAD), lambda i, h, j: (i, j, h)),
                  pl.BlockSpec((1, s, hb * HEAD_PAD), lambda i, h, j: (i, 0, h)),
                  pl.BlockSpec((1, hb * HEAD_PAD, s), lambda i, h, j: (i, h, 0))],
        out_specs=pl.BlockSpec((1, hb * V_DIM, tq), lambda i, h, j: (i, h, j)),
        out_shape=jax.ShapeDtypeStruct((b, MLA_HEADS * V_DIM, s), BF),
        compiler_params=_params(("parallel", "parallel", "arbitrary")),
        name="mla_attn",
    )(q3, k3, vt3)


def _sgu_kernel(n_ref, wc_ref, lg_ref, lb_ref, ws_ref, bs_ref, wr_ref, o_ref, gbuf, *, tm):
    z = _gelu(_dot(n_ref[...], wc_ref[...]))
    u = z[:, :SGU_WIDTH]
    v = z[:, SGU_WIDTH:]
    mu = jnp.mean(v, axis=-1, keepdims=True)
    vc = v - mu
    var = jnp.mean(vc * vc, axis=-1, keepdims=True)
    vn = (vc * lax.rsqrt(var + EPS) * lg_ref[...] + lb_ref[...]).astype(BF)
    t_out = lax.broadcasted_iota(jnp.int32, (SGU_LEN, SGU_LEN), 0) // CHUNK
    s_in = lax.broadcasted_iota(jnp.int32, (SGU_LEN, SGU_LEN), 1) // CHUNK
    mask = s_in <= t_out
    for g in range(SGU_GROUPS):
        w = jnp.where(mask, ws_ref[g], 0.0).astype(BF)
        bias = bs_ref[:, g:g + 1]
        cs = slice(g * SGU_GROUP, (g + 1) * SGU_GROUP)
        for r in range(tm // SGU_LEN):
            rs = slice(r * SGU_LEN, (r + 1) * SGU_LEN)
            sp = _dot(w, vn[rs, cs]) + bias
            gbuf[rs, cs] = (u[rs, cs] * sp).astype(BF)
    o_ref[...] = _dot(gbuf[...], wr_ref[...]).astype(BF)


def _sgu(n2, wc, ln_g, ln_b, w_s, b_st, w_read, seq):
    n_tok = n2.shape[0]
    tm = min(TM_TOK, seq)
    tok = lambda i: (i, 0)
    ws = [wc, ln_g, ln_b, w_s, b_st, w_read]
    return pl.pallas_call(
        functools.partial(_sgu_kernel, tm=tm),
        grid=(n_tok // tm,),
        in_specs=[pl.BlockSpec((tm, D_MODEL), tok)] + [_wspec(w.shape) for w in ws],
        out_specs=pl.BlockSpec((tm, D_MODEL), tok),
        out_shape=jax.ShapeDtypeStruct((n_tok, D_MODEL), BF),
        scratch_shapes=[pltpu.VMEM((tm, SGU_WIDTH), BF)],
        compiler_params=_params(("parallel",)),
        name="sgu",
    )(n2, *ws)


def _merge_kernel(x_ref, n_ref, ya_ref, obt_ref, yc_ref, wg_ref, gb_ref, wrb_ref, wo_ref, o_ref):
    g = jax.nn.sigmoid(_dot(n_ref[...], wg_ref[...]) + gb_ref[...])
    yb = lax.dot_general(obt_ref[0], wrb_ref[...], (((0,), (0,)), ((), ())), preferred_element_type=F32)
    merged = (g[:, 0:D_MODEL] * ya_ref[...].astype(F32) + g[:, D_MODEL:2 * D_MODEL] * yb
              + g[:, 2 * D_MODEL:3 * D_MODEL] * yc_ref[...].astype(F32))
    o_ref[...] = x_ref[...] + _dot(merged.astype(BF), wo_ref[...])


def _merge(x, n2, ya, obt, yc, wg, gb, w_read_b, w_out):
    n_tok = x.shape[0]
    seq = obt.shape[2]
    tm = min(TM_TOK, seq)
    per_seq = seq // tm
    tok = lambda i: (i, 0)
    tok_spec = pl.BlockSpec((tm, D_MODEL), tok)
    obt_spec = pl.BlockSpec((1, obt.shape[1], tm), lambda i: (i // per_seq, 0, i % per_seq))
    ws = [wg, gb, w_read_b, w_out]
    return pl.pallas_call(
        _merge_kernel,
        grid=(n_tok // tm,),
        in_specs=[tok_spec, tok_spec, tok_spec, obt_spec, tok_spec] + [_wspec(w.shape) for w in ws],
        out_specs=pl.BlockSpec((tm, D_MODEL), tok),
        out_shape=jax.ShapeDtypeStruct((n_tok, D_MODEL), F32),
        compiler_params=_params(("parallel",)),
        name="merge",
    )(x, n2, ya, obt, yc, *ws)


def _rope_tables(seq):
    half = QK_ROPE // 2
    inv_freq = ROPE_THETA ** (-jnp.arange(half, dtype=F32) / half)
    ang = jnp.arange(seq, dtype=F32)[:, None] * inv_freq[None, :]
    cos, sin = jnp.cos(ang), jnp.sin(ang)
    ones = jnp.ones((seq, QK_NOPE), F32)
    zpad = jnp.zeros((seq, HEAD_PAD - QK_NOPE - QK_ROPE), F32)
    cos_t = jnp.concatenate([ones, cos, cos, zpad], axis=1)
    sin_t = jnp.concatenate([jnp.zeros((seq, QK_NOPE), F32), sin, sin, zpad], axis=1)
    return cos_t, sin_t


def _rot_half_cols(w):
    half = QK_ROPE // 2
    return jnp.concatenate([-w[..., half:], w[..., :half]], axis=-1)


def _pad_head(nope, rope):
    width = nope.shape[-1] + rope.shape[-1]
    z = jnp.zeros(nope.shape[:-1] + (HEAD_PAD - width,), nope.dtype)
    return jnp.concatenate([nope, rope, z], axis=-1)


def _block_diag_tiles(w):
    per = LRU_TILE // RNN_BLOCK
    w4 = w.reshape(-1, per, RNN_BLOCK, RNN_BLOCK)
    eye = jnp.eye(per, dtype=w.dtype)
    return jnp.einsum("jacd,ab->jacbd", w4, eye).reshape(-1, LRU_TILE, LRU_TILE)


def kernel(x, p, ffn1_norm, ffn1_w_gu, ffn1_w_down, mix_norm, w_in, conv_w, conv_b, lru_w_a, lru_b_a, lru_w_x, lru_b_x, lru_lambda, w_read_a, mla_q_norm, mla_w_uq, mla_kv_norm, mla_w_ukv, w_read_b, sgu_norm_g, sgu_norm_b, sgu_w_s, sgu_b_s, w_read_c, gate_bias, w_out, ffn2_norm, ffn2_w_gu, ffn2_w_down, ple_norm, ple_w_gate, ple_w_proj, final_norm):
    b, s, d = x.shape
    depth = p.shape[0]
    n_tok = b * s
    row = lambda a: a.reshape(1, -1)
    cos_t, sin_t = _rope_tables(s)
    xf = x.reshape(n_tok, d)
    for i in range(depth):
        wi = w_in[i]
        xf, n2 = _ffn1(xf, row(ffn1_norm[i]), ffn1_w_gu[i][:, :D_FF].astype(BF), ffn1_w_gu[i][:, D_FF:].astype(BF),
                       ffn1_w_down[i].astype(BF), row(mix_norm[i]))
        n3 = n2.reshape(b, s, d)

        ya = _mixer_a(n3, wi[:, _O_XA:_O_GA].astype(BF), wi[:, _O_GA:_O_CQ].astype(BF), conv_w[i], row(conv_b[i]),
                      _block_diag_tiles(lru_w_a[i]).astype(BF), row(lru_b_a[i]),
                      _block_diag_tiles(lru_w_x[i]).astype(BF), row(lru_b_x[i]), row(lru_lambda[i]),
                      w_read_a[i].astype(BF)).reshape(n_tok, d)

        w_kr = wi[:, _O_KR:_O_ZC]
        zn = jnp.zeros((d, QK_NOPE), F32)
        wb = jnp.concatenate([wi[:, _O_CQ:_O_KR], _pad_head(zn, w_kr), _pad_head(zn, _rot_half_cols(w_kr))],
                             axis=1).astype(BF)
        wq = mla_w_uq[i].reshape(Q_LORA, MLA_HEADS, QK_NOPE + QK_ROPE)
        wq_n, wq_r = wq[..., :QK_NOPE], wq[..., QK_NOPE:]
        wq_pad = _pad_head(wq_n, wq_r).reshape(Q_LORA, -1).astype(BF)
        wq_rot = _pad_head(jnp.zeros_like(wq_n), _rot_half_cols(wq_r)).reshape(Q_LORA, -1).astype(BF)
        wkv = mla_w_ukv[i].reshape(KV_LORA, MLA_HEADS, QK_NOPE + V_DIM)
        wk_pad = _pad_head(wkv[..., :QK_NOPE], jnp.zeros((KV_LORA, MLA_HEADS, QK_ROPE), F32))
        wk_pad = wk_pad.reshape(KV_LORA, -1).astype(BF)
        wv = wkv[..., QK_NOPE:]
        wv_pad = _pad_head(wv, jnp.zeros((KV_LORA, MLA_HEADS, 0), F32))
        wvt = wv_pad.reshape(KV_LORA, -1).T.astype(BF)
        q2, k2, vt3 = _mla_proj(n2, wb, row(mla_q_norm[i]), row(mla_kv_norm[i]), wq_pad, wq_rot, wk_pad, wvt,
                                cos_t, sin_t, s)
        ob = _mla_attn(q2.reshape(b, s, -1), k2.reshape(b, s, -1), vt3)

        yc = _sgu(n2, wi[:, _O_ZC:_O_GL].astype(BF), row(sgu_norm_g[i]), row(sgu_norm_b[i]), sgu_w_s[i],
                  sgu_b_s[i].T, w_read_c[i].astype(BF), s)

        xf = _merge(xf, n2, ya, ob, yc, wi[:, _O_GL:].astype(BF), row(gate_bias[i]), w_read_b[i].astype(BF),
                    w_out[i].astype(BF))

        xf = _ffn2(xf, row(ffn2_norm[i]), ffn2_w_gu[i][:, :D_FF].astype(BF), ffn2_w_gu[i][:, D_FF:].astype(BF),
                   ffn2_w_down[i].astype(BF), p[i].reshape(n_tok, PLE_DIM), row(ple_norm[i]),
                   ple_w_gate[i].astype(BF), ple_w_proj[i].astype(BF), row(final_norm), final=(i == depth - 1))
    return xf.reshape(b, s, d)
```

```python
import functools

import jax
import jax.numpy as jnp
from jax import lax
from jax.experimental import pallas as pl
from jax.experimental.pallas import tpu as pltpu

F32 = jnp.float32
BF = jnp.bfloat16

D_MODEL = 1024
D_FF = 2816
PLE_DIM = 256
EPS = 1e-6
CHUNK = 64

RNN_WIDTH = 1024
RNN_BLOCK = 64
CONV_WIDTH = 4
LRU_C = 8.0
LRU_TILE = 256

MLA_HEADS = 16
Q_LORA = 384
KV_LORA = 256
QK_NOPE = 64
QK_ROPE = 32
V_DIM = 64
ROPE_THETA = 10000.0
HEAD_PAD = 128

SGU_WIDTH = 1024
SGU_GROUPS = 8
SGU_GROUP = 128
SGU_LEN = 128

_O_XA, _O_GA, _O_CQ, _O_CKV, _O_KR, _O_ZC, _O_GL = 0, 1024, 2048, 2432, 2688, 2720, 4768
W_IN_COLS = 7840

VMEM_LIMIT = 56 * 1024 * 1024

FF_CHUNK = 256
TM_FFN = 1024
TM_TOK = 512
TS_A = 512
TQ = 512
ATTN_HEADS_PER_STEP = 4
LOG2_E = 1.4426950408889634


def _wspec(shape):
    nd = len(shape)
    return pl.BlockSpec(shape, lambda *_: (0,) * nd, pipeline_mode=pl.Buffered(1))


def _params(sem):
    return pltpu.CompilerParams(dimension_semantics=sem, vmem_limit_bytes=VMEM_LIMIT)


def _rms(x, g):
    return x * lax.rsqrt(jnp.mean(x * x, axis=-1, keepdims=True) + EPS) * g


def _dot(a, b):
    return jnp.dot(a, b, preferred_element_type=F32)


def _sigmoid(x):
    return 0.5 * jnp.tanh(0.5 * x) + 0.5


def _gelu(x):
    return 0.5 * x * (1.0 + jnp.tanh(0.7978845608028654 * (x + 0.044715 * (x * x * x))))


def _swiglu_residual(x, g_ref, wg_ref, wu_ref, wd_ref, acc_ref):
    nb = _rms(x, g_ref[...]).astype(BF)
    for c in range(D_FF // FF_CHUNK):
        sl = slice(c * FF_CHUNK, (c + 1) * FF_CHUNK)
        gate = _dot(nb, wg_ref[:, sl])
        up = _dot(nb, wu_ref[:, sl])
        h = (gate * jax.nn.sigmoid(gate) * up).astype(BF)
        d = _dot(h, wd_ref[sl, :])
        if c == 0:
            acc_ref[...] = d
        else:
            acc_ref[...] += d
    return x + 0.5 * acc_ref[...]


def _ffn1_kernel(x_ref, g_ref, wg_ref, wu_ref, wd_ref, mg_ref, o_ref, n_ref, acc_ref):
    y = _swiglu_residual(x_ref[...], g_ref, wg_ref, wu_ref, wd_ref, acc_ref)
    o_ref[...] = y
    n_ref[...] = _rms(y, mg_ref[...]).astype(BF)


def _ffn1(x, g, wg, wu, wd, mix_g):
    n_tok = x.shape[0]
    tm = min(TM_FFN, n_tok)
    tok = lambda i: (i, 0)
    return pl.pallas_call(
        _ffn1_kernel,
        grid=(n_tok // tm,),
        in_specs=[pl.BlockSpec((tm, D_MODEL), tok), _wspec(g.shape), _wspec(wg.shape), _wspec(wu.shape),
                  _wspec(wd.shape), _wspec(mix_g.shape)],
        out_specs=[pl.BlockSpec((tm, D_MODEL), tok), pl.BlockSpec((tm, D_MODEL), tok)],
        out_shape=[jax.ShapeDtypeStruct((n_tok, D_MODEL), F32), jax.ShapeDtypeStruct((n_tok, D_MODEL), BF)],
        scratch_shapes=[pltpu.VMEM((tm, D_MODEL), F32)],
        compiler_params=_params(("parallel",)),
        name="ffn1",
    )(x, g, wg, wu, wd, mix_g)


def _ffn2_kernel(x_ref, g_ref, wg_ref, wu_ref, wd_ref, p_ref, pg_ref, wpg_ref, wpp_ref, fg_ref, o_ref, acc_ref,
                 *, final):
    y = _swiglu_residual(x_ref[...], g_ref, wg_ref, wu_ref, wd_ref, acc_ref)
    gate = jax.nn.sigmoid(_dot(_rms(y, pg_ref[...]).astype(BF), wpg_ref[...]))
    y = y + _dot(p_ref[...].astype(BF), wpp_ref[...]) * gate
    if final:
        y = _rms(y, fg_ref[...])
    o_ref[...] = y


def _ffn2(x, g, wg, wu, wd, p, ple_g, w_pg, w_pp, final_g, final):
    n_tok = x.shape[0]
    tm = min(TM_FFN, n_tok)
    tok = lambda i: (i, 0)
    return pl.pallas_call(
        functools.partial(_ffn2_kernel, final=final),
        grid=(n_tok // tm,),
        in_specs=[pl.BlockSpec((tm, D_MODEL), tok), _wspec(g.shape), _wspec(wg.shape), _wspec(wu.shape),
                  _wspec(wd.shape), pl.BlockSpec((tm, PLE_DIM), tok), _wspec(ple_g.shape), _wspec(w_pg.shape),
                  _wspec(w_pp.shape), _wspec(final_g.shape)],
        out_specs=pl.BlockSpec((tm, D_MODEL), tok),
        out_shape=jax.ShapeDtypeStruct((n_tok, D_MODEL), F32),
        scratch_shapes=[pltpu.VMEM((tm, D_MODEL), F32)],
        compiler_params=_params(("parallel",)),
        name="ffn2_ple",
    )(x, g, wg, wu, wd, p, ple_g, w_pg, w_pp, final_g)


def _mixer_a_kernel(n_ref, wxa_ref, wga_ref, cw_ref, cb_ref, wa_ref, ba_ref, wx_ref, bx_ref, lam_ref, wr_ref,
                    o_ref, xbuf, abuf, ubuf, hcar, *, ts):
    t = pl.program_id(1)

    @pl.when(t == 0)
    def _():
        xbuf[0:8, :] = jnp.zeros((8, RNN_WIDTH), F32)
        hcar[...] = jnp.zeros((8, RNN_WIDTH), F32)

    nb = n_ref[0]
    xbuf[8:8 + ts, :] = _dot(nb, wxa_ref[...])
    xc = cb_ref[...] + xbuf[8:8 + ts, :] * cw_ref[3:4, :]
    for k in range(CONV_WIDTH - 1):
        xc = xc + xbuf[5 + k:5 + k + ts, :] * cw_ref[k:k + 1, :]
    xbuf[0:8, :] = xbuf[ts:ts + 8, :]

    xcb = xc.astype(BF)
    ra, ia = [], []
    for j in range(RNN_WIDTH // LRU_TILE):
        sl = slice(j * LRU_TILE, (j + 1) * LRU_TILE)
        ra.append(_dot(xcb[:, sl], wa_ref[j]))
        ia.append(_dot(xcb[:, sl], wx_ref[j]))
    r = _sigmoid(jnp.concatenate(ra, axis=1) + ba_ref[...])
    gi = _sigmoid(jnp.concatenate(ia, axis=1) + bx_ref[...])
    neg_lam = -lam_ref[...]
    softplus = jnp.maximum(neg_lam, 0.0) + jnp.log1p(jnp.exp(-jnp.abs(neg_lam)))
    log_a = r * ((-LRU_C) * softplus)
    a = jnp.exp(log_a)
    th = jnp.tanh(log_a)
    mult = jnp.sqrt(-2.0 * th / (1.0 - th))
    row = lax.broadcasted_iota(jnp.int32, (ts, 1), 0) + t * ts
    mult = jnp.where(row == 0, 1.0, mult)
    abuf[...] = a
    ubuf[...] = mult * (gi * xc)

    sub = lax.broadcasted_iota(jnp.int32, (8, RNN_WIDTH), 0)

    def group(i, h_prev):
        r0 = pl.multiple_of(i * 8, 8)
        av = abuf[pl.ds(r0, 8), :]
        bv = ubuf[pl.ds(r0, 8), :]
        for d in (1, 2, 4):
            keep = sub >= d
            a_sh = jnp.where(keep, pltpu.roll(av, d, 0), 1.0)
            b_sh = jnp.where(keep, pltpu.roll(bv, d, 0), 0.0)
            bv = av * b_sh + bv
            av = av * a_sh
        h = av * h_prev + bv
        ubuf[pl.ds(r0, 8), :] = h
        return jnp.broadcast_to(h[7:8, :], (8, RNN_WIDTH))

    hcar[...] = lax.fori_loop(0, ts // 8, group, hcar[...], unroll=True)

    ga = _dot(nb, wga_ref[...])
    o_ref[0] = _dot((ubuf[...] * _gelu(ga)).astype(BF), wr_ref[...]).astype(BF)


def _mixer_a(n3, wxa, wga, conv_w, conv_b, wa, ba, wx, bx, lam, w_read):
    b, s, _ = n3.shape
    ts = min(TS_A, s)
    tok = lambda i, j: (i, j, 0)
    ws = [wxa, wga, conv_w, conv_b, wa, ba, wx, bx, lam, w_read]
    return pl.pallas_call(
        functools.partial(_mixer_a_kernel, ts=ts),
        grid=(b, s // ts),
        in_specs=[pl.BlockSpec((1, ts, D_MODEL), tok)] + [_wspec(w.shape) for w in ws],
        out_specs=pl.BlockSpec((1, ts, D_MODEL), tok),
        out_shape=jax.ShapeDtypeStruct((b, s, D_MODEL), BF),
        scratch_shapes=[pltpu.VMEM((ts + 8, RNN_WIDTH), F32), pltpu.VMEM((ts, RNN_WIDTH), F32),
                        pltpu.VMEM((ts, RNN_WIDTH), F32), pltpu.VMEM((8, RNN_WIDTH), F32)],
        compiler_params=_params(("parallel", "arbitrary")),
        name="mixer_a",
    )(n3, *ws)


def _mla_proj_kernel(n_ref, wb_ref, qg_ref, kg_ref, wq_ref, wqr_ref, wk_ref, wvt_ref, cos_ref, sin_ref,
                     q_ref, k_ref, vt_ref):
    c = _dot(n_ref[...], wb_ref[...])
    cq = _rms(c[:, 0:Q_LORA], qg_ref[...]).astype(BF)
    ckv = _rms(c[:, Q_LORA:Q_LORA + KV_LORA], kg_ref[...]).astype(BF)
    o_kr = Q_LORA + KV_LORA
    cos = cos_ref[...]
    sin = sin_ref[...]
    k_rope = c[:, o_kr:o_kr + HEAD_PAD] * cos + c[:, o_kr + HEAD_PAD:o_kr + 2 * HEAD_PAD] * sin
    scale = (QK_NOPE + QK_ROPE) ** -0.5 * LOG2_E
    q = _dot(cq, wq_ref[...])
    qr = _dot(cq, wqr_ref[...])
    k = _dot(ckv, wk_ref[...])
    for h in range(MLA_HEADS):
        sl = slice(h * HEAD_PAD, (h + 1) * HEAD_PAD)
        q_ref[:, sl] = ((q[:, sl] * cos + qr[:, sl] * sin) * scale).astype(BF)
        k_ref[:, sl] = (k[:, sl] + k_rope).astype(BF)
    vt = lax.dot_general(wvt_ref[...], ckv, (((1,), (1,)), ((), ())), preferred_element_type=F32)
    head_row = lax.broadcasted_iota(jnp.int32, vt.shape, 0) % HEAD_PAD
    vt_ref[0] = jnp.where(head_row == V_DIM, 1.0, vt).astype(BF)


def _mla_proj(n2, wb, q_g, kv_g, wq, wqr, wk, wvt, cos_t, sin_t, seq):
    n_tok = n2.shape[0]
    tm = min(TM_TOK, seq)
    per_seq = seq // tm
    tok = lambda i: (i, 0)
    pos = lambda i: (i % per_seq, 0)
    ws = [wb, q_g, kv_g, wq, wqr, wk, wvt]
    hp = MLA_HEADS * HEAD_PAD
    return pl.pallas_call(
        _mla_proj_kernel,
        grid=(n_tok // tm,),
        in_specs=[pl.BlockSpec((tm, D_MODEL), tok)] + [_wspec(w.shape) for w in ws]
        + [pl.BlockSpec((tm, HEAD_PAD), pos), pl.BlockSpec((tm, HEAD_PAD), pos)],
        out_specs=[pl.BlockSpec((tm, hp), tok), pl.BlockSpec((tm, hp), tok),
                   pl.BlockSpec((1, hp, tm), lambda i: (i // per_seq, 0, i % per_seq))],
        out_shape=[jax.ShapeDtypeStruct((n_tok, hp), BF), jax.ShapeDtypeStruct((n_tok, hp), BF),
                   jax.ShapeDtypeStruct((n_tok // seq, hp, seq), BF)],
        compiler_params=_params(("parallel",)),
        name="mla_proj",
    )(n2, *ws, cos_t, sin_t)


def _mla_attn_kernel(q_ref, k_ref, vt_ref, o_ref, *, tq, nq):
    qi = pl.program_id(2)
    key_chunk = lax.broadcasted_iota(jnp.int32, (tq, tq), 0) // CHUNK
    qry_chunk = lax.broadcasted_iota(jnp.int32, (tq, tq), 1) // CHUNK
    diag_mask = key_chunk <= qry_chunk
    nt = (((1,), (1,)), ((), ()))

    for c in range(nq):
        @pl.when(qi == c)
        def _():
            d0 = c * tq
            scores = []
            for hh in range(ATTN_HEADS_PER_STEP):
                hs = slice(hh * HEAD_PAD, (hh + 1) * HEAD_PAD)
                q = q_ref[0, :, hs]
                s_d = lax.dot_general(k_ref[0, d0:d0 + tq, hs], q, nt, preferred_element_type=F32)
                s_d = jnp.where(diag_mask, s_d, -1e30)
                m = jnp.max(s_d, axis=0, keepdims=True)
                s_o = None
                if c > 0:
                    s_o = lax.dot_general(k_ref[0, 0:d0, hs], q, nt, preferred_element_type=F32)
                    m = jnp.maximum(m, jnp.max(s_o, axis=0, keepdims=True))
                scores.append((s_d, s_o, m))
            for hh in range(ATTN_HEADS_PER_STEP):
                hs = slice(hh * HEAD_PAD, (hh + 1) * HEAD_PAD)
                s_d, s_o, m = scores[hh]
                acc = _dot(vt_ref[0, hs, d0:d0 + tq], jnp.exp2(s_d - m).astype(BF))
                if c > 0:
                    acc = acc + _dot(vt_ref[0, hs, 0:d0], jnp.exp2(s_o - m).astype(BF))
                o_t = acc[0:V_DIM, :] * (1.0 / acc[V_DIM:V_DIM + 1, :])
                o_ref[0, hh * V_DIM:(hh + 1) * V_DIM, :] = o_t.astype(BF)


def _mla_attn(q3, k3, vt3):
    b, s, _ = q3.shape
    tq = min(TQ, s)
    nq = s // tq
    hb = ATTN_HEADS_PER_STEP
    return pl.pallas_call(
        functools.partial(_mla_attn_kernel, tq=tq, nq=nq),
        grid=(b, MLA_HEADS // hb, nq),
        in_specs=[pl.BlockSpec((1, tq, hb * HEAD_PAD), lambda i, h, j: (i, j, h)),
                  pl.BlockSpec((1, s, hb * HEAD_PAD), lambda i, h, j: (i, 0, h)),
                  pl.BlockSpec((1, hb * HEAD_PAD, s), lambda i, h, j: (i, h, 0))],
        out_specs=pl.BlockSpec((1, hb * V_DIM, tq), lambda i, h, j: (i, h, j)),
        out_shape=jax.ShapeDtypeStruct((b, MLA_HEADS * V_DIM, s), BF),
        compiler_params=_params(("parallel", "parallel", "arbitrary")),
        name="mla_attn",
    )(q3, k3, vt3)


def _sgu_kernel(n_ref, wc_ref, lg_ref, lb_ref, ws_ref, bs_ref, wr_ref, o_ref, gbuf, *, tm):
    z = _gelu(_dot(n_ref[...], wc_ref[...]))
    u = z[:, :SGU_WIDTH]
    v = z[:, SGU_WIDTH:]
    mu = jnp.mean(v, axis=-1, keepdims=True)
    vc = v - mu
    var = jnp.mean(vc * vc, axis=-1, keepdims=True)
    vn = (vc * lax.rsqrt(var + EPS) * lg_ref[...] + lb_ref[...]).astype(BF)
    t_out = lax.broadcasted_iota(jnp.int32, (SGU_LEN, SGU_LEN), 0) // CHUNK
    s_in = lax.broadcasted_iota(jnp.int32, (SGU_LEN, SGU_LEN), 1) // CHUNK
    mask = s_in <= t_out
    for g in range(SGU_GROUPS):
        w = jnp.where(mask, ws_ref[g], 0.0).astype(BF)
        bias = bs_ref[:, g:g + 1]
        cs = slice(g * SGU_GROUP, (g + 1) * SGU_GROUP)
        for r in range(tm // SGU_LEN):
            rs = slice(r * SGU_LEN, (r + 1) * SGU_LEN)
            sp = _dot(w, vn[rs, cs]) + bias
            gbuf[rs, cs] = (u[rs, cs] * sp).astype(BF)
    o_ref[...] = _dot(gbuf[...], wr_ref[...]).astype(BF)


def _sgu(n2, wc, ln_g, ln_b, w_s, b_st, w_read, seq):
    n_tok = n2.shape[0]
    tm = min(TM_TOK, seq)
    tok = lambda i: (i, 0)
    ws = [wc, ln_g, ln_b, w_s, b_st, w_read]
    return pl.pallas_call(
        functools.partial(_sgu_kernel, tm=tm),
        grid=(n_tok // tm,),
        in_specs=[pl.BlockSpec((tm, D_MODEL), tok)] + [_wspec(w.shape) for w in ws],
        out_specs=pl.BlockSpec((tm, D_MODEL), tok),
        out_shape=jax.ShapeDtypeStruct((n_tok, D_MODEL), BF),
        scratch_shapes=[pltpu.VMEM((tm, SGU_WIDTH), BF)],
        compiler_params=_params(("parallel",)),
        name="sgu",
    )(n2, *ws)


def _merge_kernel(x_ref, n_ref, ya_ref, obt_ref, yc_ref, wg_ref, gb_ref, wrb_ref, wo_ref, o_ref):
    g = jax.nn.sigmoid(_dot(n_ref[...], wg_ref[...]) + gb_ref[...])
    yb = lax.dot_general(obt_ref[0], wrb_ref[...], (((0,), (0,)), ((), ())), preferred_element_type=F32)
    merged = (g[:, 0:D_MODEL] * ya_ref[...].astype(F32) + g[:, D_MODEL:2 * D_MODEL] * yb
              + g[:, 2 * D_MODEL:3 * D_MODEL] * yc_ref[...].astype(F32))
    o_ref[...] = x_ref[...] + _dot(merged.astype(BF), wo_ref[...])


def _merge(x, n2, ya, obt, yc, wg, gb, w_read_b, w_out):
    n_tok = x.shape[0]
    seq = obt.shape[2]
    tm = min(TM_TOK, seq)
    per_seq = seq // tm
    tok = lambda i: (i, 0)
    tok_spec = pl.BlockSpec((tm, D_MODEL), tok)
    obt_spec = pl.BlockSpec((1, obt.shape[1], tm), lambda i: (i // per_seq, 0, i % per_seq))
    ws = [wg, gb, w_read_b, w_out]
    return pl.pallas_call(
        _merge_kernel,
        grid=(n_tok // tm,),
        in_specs=[tok_spec, tok_spec, tok_spec, obt_spec, tok_spec] + [_wspec(w.shape) for w in ws],
        out_specs=pl.BlockSpec((tm, D_MODEL), tok),
        out_shape=jax.ShapeDtypeStruct((n_tok, D_MODEL), F32),
        compiler_params=_params(("parallel",)),
        name="merge",
    )(x, n2, ya, obt, yc, *ws)


def _rope_tables(seq):
    half = QK_ROPE // 2
    inv_freq = ROPE_THETA ** (-jnp.arange(half, dtype=F32) / half)
    ang = jnp.arange(seq, dtype=F32)[:, None] * inv_freq[None, :]
    cos, sin = jnp.cos(ang), jnp.sin(ang)
    ones = jnp.ones((seq, QK_NOPE), F32)
    zpad = jnp.zeros((seq, HEAD_PAD - QK_NOPE - QK_ROPE), F32)
    cos_t = jnp.concatenate([ones, cos, cos, zpad], axis=1)
    sin_t = jnp.concatenate([jnp.zeros((seq, QK_NOPE), F32), sin, sin, zpad], axis=1)
    return cos_t, sin_t


def _rot_half_cols(w):
    half = QK_ROPE // 2
    return jnp.concatenate([-w[..., half:], w[..., :half]], axis=-1)


def _pad_head(nope, rope):
    width = nope.shape[-1] + rope.shape[-1]
    z = jnp.zeros(nope.shape[:-1] + (HEAD_PAD - width,), nope.dtype)
    return jnp.concatenate([nope, rope, z], axis=-1)


def _block_diag_tiles(w):
    per = LRU_TILE // RNN_BLOCK
    w4 = w.reshape(-1, per, RNN_BLOCK, RNN_BLOCK)
    eye = jnp.eye(per, dtype=w.dtype)
    return jnp.einsum("jacd,ab->jacbd", w4, eye).reshape(-1, LRU_TILE, LRU_TILE)


def kernel(x, p, ffn1_norm, ffn1_w_gu, ffn1_w_down, mix_norm, w_in, conv_w, conv_b, lru_w_a, lru_b_a, lru_w_x, lru_b_x, lru_lambda, w_read_a, mla_q_norm, mla_w_uq, mla_kv_norm, mla_w_ukv, w_read_b, sgu_norm_g, sgu_norm_b, sgu_w_s, sgu_b_s, w_read_c, gate_bias, w_out, ffn2_norm, ffn2_w_gu, ffn2_w_down, ple_norm, ple_w_gate, ple_w_proj, final_norm):
    b, s, d = x.shape
    depth = p.shape[0]
    n_tok = b * s
    row = lambda a: a.reshape(1, -1)
    cos_t, sin_t = _rope_tables(s)
    xf = x.reshape(n_tok, d)
    for i in range(depth):
        wi = w_in[i]
        xf, n2 = _ffn1(xf, row(ffn1_norm[i]), ffn1_w_gu[i][:, :D_FF].astype(BF), ffn1_w_gu[i][:, D_FF:].astype(BF),
                       ffn1_w_down[i].astype(BF), row(mix_norm[i]))
        n3 = n2.reshape(b, s, d)

        ya = _mixer_a(n3, wi[:, _O_XA:_O_GA].astype(BF), wi[:, _O_GA:_O_CQ].astype(BF), conv_w[i], row(conv_b[i]),
                      _block_diag_tiles(lru_w_a[i]).astype(BF), row(lru_b_a[i]),
                      _block_diag_tiles(lru_w_x[i]).astype(BF), row(lru_b_x[i]), row(lru_lambda[i]),
                      w_read_a[i].astype(BF)).reshape(n_tok, d)

        w_kr = wi[:, _O_KR:_O_ZC]
        zn = jnp.zeros((d, QK_NOPE), F32)
        wb = jnp.concatenate([wi[:, _O_CQ:_O_KR], _pad_head(zn, w_kr), _pad_head(zn, _rot_half_cols(w_kr))],
                             axis=1).astype(BF)
        wq = mla_w_uq[i].reshape(Q_LORA, MLA_HEADS, QK_NOPE + QK_ROPE)
        wq_n, wq_r = wq[..., :QK_NOPE], wq[..., QK_NOPE:]
        wq_pad = _pad_head(wq_n, wq_r).reshape(Q_LORA, -1).astype(BF)
        wq_rot = _pad_head(jnp.zeros_like(wq_n), _rot_half_cols(wq_r)).reshape(Q_LORA, -1).astype(BF)
        wkv = mla_w_ukv[i].reshape(KV_LORA, MLA_HEADS, QK_NOPE + V_DIM)
        wk_pad = _pad_head(wkv[..., :QK_NOPE], jnp.zeros((KV_LORA, MLA_HEADS, QK_ROPE), F32))
        wk_pad = wk_pad.reshape(KV_LORA, -1).astype(BF)
        wv = wkv[..., QK_NOPE:]
        wv_pad = _pad_head(wv, jnp.zeros((KV_LORA, MLA_HEADS, 0), F32))
        wvt = wv_pad.reshape(KV_LORA, -1).T.astype(BF)
        q2, k2, vt3 = _mla_proj(n2, wb, row(mla_q_norm[i]), row(mla_kv_norm[i]), wq_pad, wq_rot, wk_pad, wvt,
                                cos_t, sin_t, s)
        ob = _mla_attn(q2.reshape(b, s, -1), k2.reshape(b, s, -1), vt3)

        yc = _sgu(n2, wi[:, _O_ZC:_O_GL].astype(BF), row(sgu_norm_g[i]), row(sgu_norm_b[i]), sgu_w_s[i],
                  sgu_b_s[i].T, w_read_c[i].astype(BF), s)

        xf = _merge(xf, n2, ya, ob, yc, wi[:, _O_GL:].astype(BF), row(gate_bias[i]), w_read_b[i].astype(BF),
                    w_out[i].astype(BF))

        xf = _ffn2(xf, row(ffn2_norm[i]), ffn2_w_gu[i][:, :D_FF].astype(BF), ffn2_w_gu[i][:, D_FF:].astype(BF),
                   ffn2_w_down[i].astype(BF), p[i].reshape(n_tok, PLE_DIM), row(ple_norm[i]),
                   ple_w_gate[i].astype(BF), ple_w_proj[i].astype(BF), row(final_norm), final=(i == depth - 1))
    return xf.reshape(b, s, d)
```

```python
import functools

import jax
import jax.numpy as jnp
from jax import lax
from jax.experimental import pallas as pl
from jax.experimental.pallas import tpu as pltpu

F32 = jnp.float32
BF = jnp.bfloat16

D_MODEL = 1024
D_FF = 2816
PLE_DIM = 256
EPS = 1e-6
CHUNK = 64

RNN_WIDTH = 1024
RNN_BLOCK = 64
CONV_WIDTH = 4
LRU_C = 8.0
LRU_TILE = 256

MLA_HEADS = 16
Q_LORA = 384
KV_LORA = 256
QK_NOPE = 64
QK_ROPE = 32
V_DIM = 64
ROPE_THETA = 10000.0
HEAD_PAD = 128

SGU_WIDTH = 1024
SGU_GROUPS = 8
SGU_GROUP = 128
SGU_LEN = 128

_O_XA, _O_GA, _O_CQ, _O_CKV, _O_KR, _O_ZC, _O_GL = 0, 1024, 2048, 2432, 2688, 2720, 4768
W_IN_COLS = 7840

VMEM_LIMIT = 56 * 1024 * 1024

FF_CHUNK = 256
TM_FFN = 1024
TM_TOK = 512
TS_A = 512
NSEQ_A = 2
TQ = 512
ATTN_HEADS_PER_STEP = 4
LOG2_E = 1.4426950408889634


def _wspec(shape):
    nd = len(shape)
    return pl.BlockSpec(shape, lambda *_: (0,) * nd, pipeline_mode=pl.Buffered(1))


def _whole(arr):
    return arr, _wspec(arr.shape)


def _layer(arr, i, cols=None, col_block=0):
    shape = arr.shape[1:]
    if cols is not None:
        shape = shape[:-1] + (cols,)
    idx = (i,) + (0,) * (len(shape) - 1) + (col_block,)
    return arr, pl.BlockSpec((None,) + shape, lambda *_: idx, pipeline_mode=pl.Buffered(1))


def _split(ws):
    return [a for a, _ in ws], [sp for _, sp in ws]


def _params(sem):
    return pltpu.CompilerParams(dimension_semantics=sem, vmem_limit_bytes=VMEM_LIMIT)


def _rms(x, g):
    return x * lax.rsqrt(jnp.mean(x * x, axis=-1, keepdims=True) + EPS) * g


def _dot(a, b):
    return jnp.dot(a, b, preferred_element_type=F32)


def _sigmoid(x):
    return 0.5 * jnp.tanh(0.5 * x) + 0.5


def _gelu(x):
    return 0.5 * x * (1.0 + jnp.tanh(0.7978845608028654 * (x + 0.044715 * (x * x * x))))


def _swiglu_residual(x, g_ref, wg_ref, wu_ref, wd_ref, acc_ref):
    nb = _rms(x, g_ref[...]).astype(BF)
    for c in range(D_FF // FF_CHUNK):
        sl = slice(c * FF_CHUNK, (c + 1) * FF_CHUNK)
        gate = _dot(nb, wg_ref[:, sl])
        up = _dot(nb, wu_ref[:, sl])
        h = (gate * jax.nn.sigmoid(gate) * up).astype(BF)
        d = _dot(h, wd_ref[sl, :])
        if c == 0:
            acc_ref[...] = d
        else:
            acc_ref[...] += d
    return x + 0.5 * acc_ref[...]


def _ffn1_kernel(x_ref, g_ref, wg_ref, wu_ref, wd_ref, mg_ref, o_ref, n_ref, acc_ref):
    y = _swiglu_residual(x_ref[...], g_ref, wg_ref, wu_ref, wd_ref, acc_ref)
    o_ref[...] = y
    n_ref[...] = _rms(y, mg_ref[...]).astype(BF)


def _ffn1(x, ws):
    n_tok = x.shape[0]
    tm = min(TM_FFN, n_tok)
    tok = lambda i: (i, 0)
    w_args, w_specs = _split(ws)
    return pl.pallas_call(
        _ffn1_kernel,
        grid=(n_tok // tm,),
        in_specs=[pl.BlockSpec((tm, D_MODEL), tok)] + w_specs,
        out_specs=[pl.BlockSpec((tm, D_MODEL), tok), pl.BlockSpec((tm, D_MODEL), tok)],
        out_shape=[jax.ShapeDtypeStruct((n_tok, D_MODEL), F32), jax.ShapeDtypeStruct((n_tok, D_MODEL), BF)],
        scratch_shapes=[pltpu.VMEM((tm, D_MODEL), F32)],
        compiler_params=_params(("parallel",)),
        name="ffn1",
    )(x, *w_args)


def _ffn2_kernel(x_ref, g_ref, wg_ref, wu_ref, wd_ref, p_ref, pg_ref, wpg_ref, wpp_ref, fg_ref, o_ref, acc_ref,
                 *, final):
    y = _swiglu_residual(x_ref[...], g_ref, wg_ref, wu_ref, wd_ref, acc_ref)
    gate = jax.nn.sigmoid(_dot(_rms(y, pg_ref[...]).astype(BF), wpg_ref[...]))
    y = y + _dot(p_ref[...].astype(BF), wpp_ref[...]) * gate
    if final:
        y = _rms(y, fg_ref[...])
    o_ref[...] = y


def _ffn2(x, ffn_ws, p3, layer, ple_ws, final):
    n_tok = x.shape[0]
    tm = min(TM_FFN, n_tok)
    tok = lambda i: (i, 0)
    f_args, f_specs = _split(ffn_ws)
    p_args, p_specs = _split(ple_ws)
    return pl.pallas_call(
        functools.partial(_ffn2_kernel, final=final),
        grid=(n_tok // tm,),
        in_specs=[pl.BlockSpec((tm, D_MODEL), tok)] + f_specs
        + [pl.BlockSpec((None, tm, PLE_DIM), lambda i: (layer, i, 0))] + p_specs,
        out_specs=pl.BlockSpec((tm, D_MODEL), tok),
        out_shape=jax.ShapeDtypeStruct((n_tok, D_MODEL), F32),
        scratch_shapes=[pltpu.VMEM((tm, D_MODEL), F32)],
        compiler_params=_params(("parallel",)),
        name="ffn2_ple",
    )(x, *f_args, p3, *p_args)


def _mixer_a_kernel(n_ref, perm_ref, unperm_ref, wxa_ref, wga_ref, cw_ref, cb_ref, wa_ref, ba_ref, wx_ref, bx_ref,
                    lam_ref, wr_ref, o_ref, tail, hcar, *, ts):
    t = pl.program_id(1)
    halo = 8 * (CONV_WIDTH - 1)
    nstep = ts // 8
    seqs = range(n_ref.shape[0])
    sub = lax.broadcasted_iota(jnp.int32, (8, RNN_WIDTH), 0)

    @pl.when(t == 0)
    def _():
        tail[...] = jnp.zeros(tail.shape, F32)
        hcar[...] = jnp.zeros(hcar.shape, F32)

    first = jnp.logical_and(sub == 0, t == 0)
    nb = [_dot(perm_ref[...], n_ref[e]).astype(BF) for e in seqs]
    xa = [_dot(nb[e], wxa_ref[...]) for e in seqs]

    xc = []
    for e in seqs:
        frames = []
        for m in range(CONV_WIDTH - 1, 0, -1):
            cur = xa[e][ts - 8 * m:ts - 8 * (m - 1), :]
            prev = tail[e, halo - 8 * m:halo - 8 * (m - 1), :]
            frames.append(pltpu.roll(jnp.where(sub == 7, prev, cur), 1, 0))
            tail[e, halo - 8 * m:halo - 8 * (m - 1), :] = cur
        ext = jnp.concatenate(frames + [xa[e]], axis=0)
        acc = cb_ref[...] + xa[e] * cw_ref[CONV_WIDTH - 1:CONV_WIDTH, :]
        for k in range(CONV_WIDTH - 1):
            acc = acc + ext[8 * k:8 * k + ts, :] * cw_ref[k:k + 1, :]
        xc.append(acc)

    gates = []
    for e in seqs:
        xcb = xc[e].astype(BF)
        ra, ia = [], []
        for j in range(RNN_WIDTH // LRU_TILE):
            sl = slice(j * LRU_TILE, (j + 1) * LRU_TILE)
            ra.append(_dot(xcb[:, sl], wa_ref[j]))
            ia.append(_dot(xcb[:, sl], wx_ref[j]))
        gates.append((jnp.concatenate(ra, axis=1), jnp.concatenate(ia, axis=1)))

    ga = [_dot(nb[e], wga_ref[...]) for e in seqs]

    neg_lam = -lam_ref[...]
    decay_rate = (-LRU_C) * (jnp.maximum(neg_lam, 0.0) + jnp.log1p(jnp.exp(-jnp.abs(neg_lam))))
    hs = []
    for e in seqs:
        r = _sigmoid(gates[e][0] + ba_ref[...])
        gi = _sigmoid(gates[e][1] + bx_ref[...])
        log_a = r * decay_rate
        th = jnp.tanh(log_a)
        m2 = -2.0 * th / (1.0 - th)
        mult = jnp.where(m2 > 0.0, m2 * lax.rsqrt(m2), 0.0)
        gx = gi * xc[e]
        a = jnp.exp(log_a)
        u = mult * gx
        h = jnp.where(first, gx[0:8, :], u[0:8, :])
        dec = a[0:8, :]
        h_loc, dec_loc = [h], [dec]
        for k in range(1, nstep):
            a_k = a[8 * k:8 * k + 8, :]
            h = a_k * h + u[8 * k:8 * k + 8, :]
            dec = a_k * dec
            h_loc.append(h)
            dec_loc.append(dec)
        start = hcar[e]
        for j in range(7):
            nxt = h + dec * start
            start = jnp.where(sub == j + 1, pltpu.roll(nxt, 1, 0), start)
        hcar[e] = jnp.broadcast_to((h + dec * start)[7:8, :], (8, RNN_WIDTH))
        hs.append(jnp.concatenate([hk + dk * start for hk, dk in zip(h_loc, dec_loc)], axis=0))

    ys = [(hs[e] * _gelu(ga[e])).astype(BF) for e in seqs]
    for e in seqs:
        y = _dot(ys[e], wr_ref[...]).astype(BF)
        o_ref[e] = _dot(unperm_ref[...], y).astype(BF)


def _segment_permutation(ts):
    r = jnp.arange(ts)
    src = (ts // 8) * (r % 8) + r // 8
    return (jnp.arange(ts)[None, :] == src[:, None]).astype(BF)


def _mixer_a(n3, ws):
    b, s, _ = n3.shape
    ts = min(TS_A, s)
    ns = NSEQ_A if b % NSEQ_A == 0 else 1
    tok = lambda i, j: (i, j, 0)
    perm = _segment_permutation(ts)
    w_args, w_specs = _split([_whole(perm), _whole(perm.T)] + ws)
    halo = 8 * (CONV_WIDTH - 1)
    return pl.pallas_call(
        functools.partial(_mixer_a_kernel, ts=ts),
        grid=(b // ns, s // ts),
        in_specs=[pl.BlockSpec((ns, ts, D_MODEL), tok)] + w_specs,
        out_specs=pl.BlockSpec((ns, ts, D_MODEL), tok),
        out_shape=jax.ShapeDtypeStruct((b, s, D_MODEL), BF),
        scratch_shapes=[pltpu.VMEM((ns, halo, RNN_WIDTH), F32), pltpu.VMEM((ns, 8, RNN_WIDTH), F32)],
        compiler_params=_params(("parallel", "arbitrary")),
        name="mixer_a",
    )(n3, *w_args)


def _mla_proj_kernel(n_ref, wb_ref, qg_ref, kg_ref, wq_ref, wk_ref, wvt_ref, cos_ref, sin_lo_ref, sin_hi_ref,
                     q_ref, k_ref, vt_ref):
    c = _dot(n_ref[...], wb_ref[...])
    cq = _rms(c[:, 0:Q_LORA], qg_ref[...]).astype(BF)
    ckv = _rms(c[:, Q_LORA:Q_LORA + KV_LORA], kg_ref[...]).astype(BF)
    o_kr = Q_LORA + KV_LORA
    cos = cos_ref[...]
    sin_lo = sin_lo_ref[...]
    sin_hi = sin_hi_ref[...]
    half = QK_ROPE // 2

    def rope(x):
        return x * cos + pltpu.roll(x, HEAD_PAD - half, 1) * sin_lo + pltpu.roll(x, half, 1) * sin_hi

    k_rope = rope(c[:, o_kr:o_kr + HEAD_PAD])
    scale = (QK_NOPE + QK_ROPE) ** -0.5 * LOG2_E
    q = _dot(cq, wq_ref[...])
    k = _dot(ckv, wk_ref[...])
    for h in range(MLA_HEADS):
        sl = slice(h * HEAD_PAD, (h + 1) * HEAD_PAD)
        q_ref[:, sl] = (rope(q[:, sl]) * scale).astype(BF)
        k_ref[:, sl] = (k[:, sl] + k_rope).astype(BF)
    vt = lax.dot_general(wvt_ref[...], ckv, (((1,), (1,)), ((), ())), preferred_element_type=F32)
    head_row = lax.broadcasted_iota(jnp.int32, vt.shape, 0) % HEAD_PAD
    vt_ref[0] = jnp.where(head_row == V_DIM, 1.0, vt).astype(BF)


def _mla_proj(n2, ws, rope_tables, seq):
    n_tok = n2.shape[0]
    tm = min(TM_TOK, seq)
    per_seq = seq // tm
    tok = lambda i: (i, 0)
    pos = lambda i: (i % per_seq, 0)
    w_args, w_specs = _split(ws)
    hp = MLA_HEADS * HEAD_PAD
    return pl.pallas_call(
        _mla_proj_kernel,
        grid=(n_tok // tm,),
        in_specs=[pl.BlockSpec((tm, D_MODEL), tok)] + w_specs
        + [pl.BlockSpec((tm, HEAD_PAD), pos)] * len(rope_tables),
        out_specs=[pl.BlockSpec((tm, hp), tok), pl.BlockSpec((tm, hp), tok),
                   pl.BlockSpec((1, hp, tm), lambda i: (i // per_seq, 0, i % per_seq))],
        out_shape=[jax.ShapeDtypeStruct((n_tok, hp), BF), jax.ShapeDtypeStruct((n_tok, hp), BF),
                   jax.ShapeDtypeStruct((n_tok // seq, hp, seq), BF)],
        compiler_params=_params(("parallel",)),
        name="mla_proj",
    )(n2, *w_args, *rope_tables)


def _mla_attn_kernel(q_ref, k_ref, vt_ref, o_ref, *, tq, nq):
    qi = pl.program_id(2)
    key_chunk = lax.broadcasted_iota(jnp.int32, (tq, tq), 0) // CHUNK
    qry_chunk = lax.broadcasted_iota(jnp.int32, (tq, tq), 1) // CHUNK
    diag_mask = key_chunk <= qry_chunk
    nt = (((1,), (1,)), ((), ()))

    for c in range(nq):
        @pl.when(qi == c)
        def _():
            d0 = c * tq
            scores = []
            for hh in range(ATTN_HEADS_PER_STEP):
                hs = slice(hh * HEAD_PAD, (hh + 1) * HEAD_PAD)
                q = q_ref[0, :, hs]
                s_d = lax.dot_general(k_ref[0, d0:d0 + tq, hs], q, nt, preferred_element_type=F32)
                s_d = jnp.where(diag_mask, s_d, -1e30)
                m = jnp.max(s_d, axis=0, keepdims=True)
                s_o = None
                if c > 0:
                    s_o = lax.dot_general(k_ref[0, 0:d0, hs], q, nt, preferred_element_type=F32)
                    m = jnp.maximum(m, jnp.max(s_o, axis=0, keepdims=True))
                scores.append((s_d, s_o, m))
            for hh in range(ATTN_HEADS_PER_STEP):
                hs = slice(hh * HEAD_PAD, (hh + 1) * HEAD_PAD)
                s_d, s_o, m = scores[hh]
                acc = _dot(vt_ref[0, hs, d0:d0 + tq], jnp.exp2(s_d - m).astype(BF))
                if c > 0:
                    acc = acc + _dot(vt_ref[0, hs, 0:d0], jnp.exp2(s_o - m).astype(BF))
                o_t = acc[0:V_DIM, :] * (1.0 / acc[V_DIM:V_DIM + 1, :])
                o_ref[0, hh * V_DIM:(hh + 1) * V_DIM, :] = o_t.astype(BF)


def _mla_attn(q3, k3, vt3):
    b, s, _ = q3.shape
    tq = min(TQ, s)
    nq = s // tq
    hb = ATTN_HEADS_PER_STEP
    return pl.pallas_call(
        functools.partial(_mla_attn_kernel, tq=tq, nq=nq),
        grid=(b, MLA_HEADS // hb, nq),
        in_specs=[pl.BlockSpec((1, tq, hb * HEAD_PAD), lambda i, h, j: (i, j, h)),
                  pl.BlockSpec((1, s, hb * HEAD_PAD), lambda i, h, j: (i, 0, h)),
                  pl.BlockSpec((1, hb * HEAD_PAD, s), lambda i, h, j: (i, h, 0))],
        out_specs=pl.BlockSpec((1, hb * V_DIM, tq), lambda i, h, j: (i, h, j)),
        out_shape=jax.ShapeDtypeStruct((b, MLA_HEADS * V_DIM, s), BF),
        compiler_params=_params(("parallel", "parallel", "arbitrary")),
        name="mla_attn",
    )(q3, k3, vt3)


def _sgu_kernel(n_ref, wc_ref, lg_ref, lb_ref, ws_ref, bs_ref, wr_ref, o_ref, gbuf, *, tm):
    z = _gelu(_dot(n_ref[...], wc_ref[...]))
    u = z[:, :SGU_WIDTH]
    v = z[:, SGU_WIDTH:]
    mu = jnp.mean(v, axis=-1, keepdims=True)
    vc = v - mu
    var = jnp.mean(vc * vc, axis=-1, keepdims=True)
    vn = (vc * lax.rsqrt(var + EPS) * lg_ref[...] + lb_ref[...]).astype(BF)
    t_out = lax.broadcasted_iota(jnp.int32, (SGU_LEN, SGU_LEN), 0) // CHUNK
    s_in = lax.broadcasted_iota(jnp.int32, (SGU_LEN, SGU_LEN), 1) // CHUNK
    mask = s_in <= t_out
    for g in range(SGU_GROUPS):
        w = jnp.where(mask, ws_ref[g], 0.0).astype(BF)
        bias = bs_ref[:, g:g + 1]
        cs = slice(g * SGU_GROUP, (g + 1) * SGU_GROUP)
        for r in range(tm // SGU_LEN):
            rs = slice(r * SGU_LEN, (r + 1) * SGU_LEN)
            sp = _dot(w, vn[rs, cs]) + bias
            gbuf[rs, cs] = (u[rs, cs] * sp).astype(BF)
    o_ref[...] = _dot(gbuf[...], wr_ref[...]).astype(BF)


def _sgu(n2, ws, seq):
    n_tok = n2.shape[0]
    tm = min(TM_TOK, seq)
    tok = lambda i: (i, 0)
    w_args, w_specs = _split(ws)
    return pl.pallas_call(
        functools.partial(_sgu_kernel, tm=tm),
        grid=(n_tok // tm,),
        in_specs=[pl.BlockSpec((tm, D_MODEL), tok)] + w_specs,
        out_specs=pl.BlockSpec((tm, D_MODEL), tok),
        out_shape=jax.ShapeDtypeStruct((n_tok, D_MODEL), BF),
        scratch_shapes=[pltpu.VMEM((tm, SGU_WIDTH), BF)],
        compiler_params=_params(("parallel",)),
        name="sgu",
    )(n2, *w_args)


def _merge_kernel(x_ref, n_ref, ya_ref, obt_ref, yc_ref, wg_ref, gb_ref, wrb_ref, wo_ref, o_ref):
    g = jax.nn.sigmoid(_dot(n_ref[...], wg_ref[...]) + gb_ref[...])
    yb = lax.dot_general(obt_ref[0], wrb_ref[...], (((0,), (0,)), ((), ())), preferred_element_type=F32)
    merged = (g[:, 0:D_MODEL] * ya_ref[...].astype(F32) + g[:, D_MODEL:2 * D_MODEL] * yb
              + g[:, 2 * D_MODEL:3 * D_MODEL] * yc_ref[...].astype(F32))
    o_ref[...] = x_ref[...] + _dot(merged.astype(BF), wo_ref[...])


def _merge(x, n2, ya, obt, yc, ws):
    n_tok = x.shape[0]
    seq = obt.shape[2]
    tm = min(TM_TOK, seq)
    per_seq = seq // tm
    tok = lambda i: (i, 0)
    tok_spec = pl.BlockSpec((tm, D_MODEL), tok)
    obt_spec = pl.BlockSpec((1, obt.shape[1], tm), lambda i: (i // per_seq, 0, i % per_seq))
    w_args, w_specs = _split(ws)
    return pl.pallas_call(
        _merge_kernel,
        grid=(n_tok // tm,),
        in_specs=[tok_spec, tok_spec, tok_spec, obt_spec, tok_spec] + w_specs,
        out_specs=pl.BlockSpec((tm, D_MODEL), tok),
        out_shape=jax.ShapeDtypeStruct((n_tok, D_MODEL), F32),
        compiler_params=_params(("parallel",)),
        name="merge",
    )(x, n2, ya, obt, yc, *w_args)


def _rope_tables(seq):
    half = QK_ROPE // 2
    inv_freq = ROPE_THETA ** (-jnp.arange(half, dtype=F32) / half)
    ang = jnp.arange(seq, dtype=F32)[:, None] * inv_freq[None, :]
    cos, sin = jnp.cos(ang), jnp.sin(ang)
    ones = jnp.ones((seq, QK_NOPE), F32)
    znope = jnp.zeros((seq, QK_NOPE), F32)
    zhalf = jnp.zeros((seq, half), F32)
    zpad = jnp.zeros((seq, HEAD_PAD - QK_NOPE - QK_ROPE), F32)
    cos_t = jnp.concatenate([ones, cos, cos, zpad], axis=1)
    sin_lo = jnp.concatenate([znope, -sin, zhalf, zpad], axis=1)
    sin_hi = jnp.concatenate([znope, zhalf, sin, zpad], axis=1)
    return cos_t, sin_lo, sin_hi


def _pad_head(nope, rope):
    width = nope.shape[-1] + rope.shape[-1]
    z = jnp.zeros(nope.shape[:-1] + (HEAD_PAD - width,), nope.dtype)
    return jnp.concatenate([nope, rope, z], axis=-1)


def _block_diag_tiles(w):
    per = LRU_TILE // RNN_BLOCK
    w4 = w.reshape(w.shape[0], -1, per, RNN_BLOCK, RNN_BLOCK)
    eye = jnp.eye(per, dtype=w.dtype)
    return jnp.einsum("ljacd,ab->ljacbd", w4, eye).reshape(w.shape[0], -1, LRU_TILE, LRU_TILE)


def kernel(x, p, ffn1_norm, ffn1_w_gu, ffn1_w_down, mix_norm, w_in, conv_w, conv_b, lru_w_a, lru_b_a, lru_w_x, lru_b_x, lru_lambda, w_read_a, mla_q_norm, mla_w_uq, mla_kv_norm, mla_w_ukv, w_read_b, sgu_norm_g, sgu_norm_b, sgu_w_s, sgu_b_s, w_read_c, gate_bias, w_out, ffn2_norm, ffn2_w_gu, ffn2_w_down, ple_norm, ple_w_gate, ple_w_proj, final_norm):
    b, s, d = x.shape
    depth = p.shape[0]
    n_tok = b * s
    rope_tables = _rope_tables(s)
    vec = lambda a: a.reshape(a.shape[0], 1, -1)

    ffn1_gu, ffn1_down = ffn1_w_gu.astype(BF), ffn1_w_down.astype(BF)
    ffn2_gu, ffn2_down = ffn2_w_gu.astype(BF), ffn2_w_down.astype(BF)
    w_in_b = w_in.astype(BF)
    w_latent = jnp.concatenate(
        [w_in_b[:, :, _O_CQ:_O_KR], _pad_head(jnp.zeros((depth, d, QK_NOPE), BF), w_in_b[:, :, _O_KR:_O_ZC])], axis=2)
    w_sgu_in = w_in_b[:, :, _O_ZC:_O_GL]
    w_gates = w_in_b[:, :, _O_GL:]
    lru_a, lru_x = _block_diag_tiles(lru_w_a).astype(BF), _block_diag_tiles(lru_w_x).astype(BF)
    wq = mla_w_uq.reshape(depth, Q_LORA, MLA_HEADS, QK_NOPE + QK_ROPE)
    wq_pad = _pad_head(wq[..., :QK_NOPE], wq[..., QK_NOPE:]).reshape(depth, Q_LORA, -1).astype(BF)
    wkv = mla_w_ukv.reshape(depth, KV_LORA, MLA_HEADS, QK_NOPE + V_DIM)
    wk_pad = _pad_head(wkv[..., :QK_NOPE], jnp.zeros((depth, KV_LORA, MLA_HEADS, QK_ROPE), F32))
    wk_pad = wk_pad.reshape(depth, KV_LORA, -1).astype(BF)
    wv_pad = _pad_head(wkv[..., QK_NOPE:], jnp.zeros((depth, KV_LORA, MLA_HEADS, 0), F32))
    wvt = jnp.swapaxes(wv_pad.reshape(depth, KV_LORA, -1), 1, 2).astype(BF)
    read_a, read_b, read_c = w_read_a.astype(BF), w_read_b.astype(BF), w_read_c.astype(BF)
    out_b, ple_gate_b, ple_proj_b = w_out.astype(BF), ple_w_gate.astype(BF), ple_w_proj.astype(BF)
    sgu_b_t = jnp.swapaxes(sgu_b_s, 1, 2)
    p3 = p.reshape(depth, n_tok, PLE_DIM)
    final_g = _whole(final_norm.reshape(1, -1))

    xf = x.reshape(n_tok, d)
    for i in range(depth):
        xf, n2 = _ffn1(xf, [_layer(vec(ffn1_norm), i), _layer(ffn1_gu, i, D_FF, 0), _layer(ffn1_gu, i, D_FF, 1),
                            _layer(ffn1_down, i), _layer(vec(mix_norm), i)])

        ya = _mixer_a(n2.reshape(b, s, d),
                      [_layer(w_in_b, i, RNN_WIDTH, 0), _layer(w_in_b, i, RNN_WIDTH, 1), _layer(conv_w, i),
                       _layer(vec(conv_b), i), _layer(lru_a, i), _layer(vec(lru_b_a), i), _layer(lru_x, i),
                       _layer(vec(lru_b_x), i), _layer(vec(lru_lambda), i), _layer(read_a, i)]).reshape(n_tok, d)

        q2, k2, vt3 = _mla_proj(n2, [_layer(w_latent, i), _layer(vec(mla_q_norm), i), _layer(vec(mla_kv_norm), i),
                                     _layer(wq_pad, i), _layer(wk_pad, i), _layer(wvt, i)], rope_tables, s)
        ob = _mla_attn(q2.reshape(b, s, -1), k2.reshape(b, s, -1), vt3)

        yc = _sgu(n2, [_layer(w_sgu_in, i), _layer(vec(sgu_norm_g), i), _layer(vec(sgu_norm_b), i),
                       _layer(sgu_w_s, i), _layer(sgu_b_t, i), _layer(read_c, i)], s)

        xf = _merge(xf, n2, ya, ob, yc,
                    [_layer(w_gates, i), _layer(vec(gate_bias), i), _layer(read_b, i), _layer(out_b, i)])

        xf = _ffn2(xf, [_layer(vec(ffn2_norm), i), _layer(ffn2_gu, i, D_FF, 0), _layer(ffn2_gu, i, D_FF, 1),
                        _layer(ffn2_down, i)], p3, i,
                   [_layer(vec(ple_norm), i), _layer(ple_gate_b, i), _layer(ple_proj_b, i), final_g],
                   final=(i == depth - 1))
    return xf.reshape(b, s, d)
```
